```python
import jax, jax.numpy as jnp
from jax import lax
import numpy as np

D_MODEL = 4096
BATCH = 1
SEQ = 8192
DEPTH = 2

CTX_LEN = 256
GRID_W = 64
POOL_WINDOWS = (2, 4, 8, 16)
POOL_WIDTH = D_MODEL // 4
POOL_GROUP = POOL_WIDTH // len(POOL_WINDOWS)
HEAD_DIM = 128
N_Q_HEADS = D_MODEL // (2 * HEAD_DIM)
N_KV_HEADS = N_Q_HEADS // 4
ATTN_WIDTH = N_Q_HEADS * HEAD_DIM
KV_WIDTH = N_KV_HEADS * HEAD_DIM
CONV_WIDTH = D_MODEL // 4
CONV_K = 3
MIX_WIDTH = POOL_WIDTH + ATTN_WIDTH + CONV_WIDTH
Q_OFF = POOL_WIDTH
K_OFF = Q_OFF + ATTN_WIDTH
V_OFF = K_OFF + KV_WIDTH
CB_OFF = V_OFF + KV_WIDTH
CC_OFF = CB_OFF + CONV_WIDTH
CX_OFF = CC_OFF + CONV_WIDTH
IN_WIDTH = CX_OFF + CONV_WIDTH
Q_BLOCK = 128
ROPE_BASE = 10000.0
D_FF_DENSE = 11008
N_EXPERTS = 8
TOP_K = 2
D_FF_EXPERT = 4096
N_MOD = 6
EPS = 1e-6
N_DENSE = (DEPTH + 1) // 2
N_MOE = DEPTH // 2

kernel_name = "hybrid_pool_gqa_conv_moe_flow_block"


def rmsnorm(x, g):
    xf = x.astype(jnp.float32)
    y = xf * lax.rsqrt(jnp.mean(xf * xf, axis=-1, keepdims=True) + EPS)
    return (y * g.astype(jnp.float32)).astype(x.dtype)


def modulate(h, shift, scale):
    return h * (1.0 + scale[:, None, :]) + shift[:, None, :]


def layer_mods(c_vec, w_mod_l, b_mod_l):
    return jnp.split(jax.nn.silu(c_vec) @ w_mod_l + b_mod_l, N_MOD, axis=-1)


def rope_tables(L):
    rows = L // GRID_W
    row = jnp.repeat(jnp.arange(rows), GRID_W).astype(jnp.float32)
    col = jnp.tile(jnp.arange(GRID_W), rows).astype(jnp.float32)
    n_axis = HEAD_DIM // 4
    inv = ROPE_BASE ** (-jnp.arange(n_axis, dtype=jnp.float32) / n_axis)
    ang = jnp.concatenate([row[:, None] * inv, col[:, None] * inv], axis=-1)
    return jnp.cos(ang), jnp.sin(ang)


def apply_rope(x, cos, sin):
    xf = x.astype(jnp.float32)
    half = HEAD_DIM // 2
    x1, x2 = xf[..., :half], xf[..., half:]
    cs, sn = cos[None, :, None, :], sin[None, :, None, :]
    return jnp.concatenate([x1 * cs - x2 * sn, x2 * cs + x1 * sn], axis=-1).astype(x.dtype)


def centred_pool_minus_self(u, w):
    L = u.shape[1]
    cs = jnp.pad(jnp.cumsum(u.astype(jnp.float32), axis=1), ((0, 0), (1, 0), (0, 0)))
    t = jnp.arange(L)
    lo = jnp.clip(t - w // 2, 0, L)
    hi = jnp.clip(t - w // 2 + w, 0, L)
    s = jnp.take(cs, hi, axis=1) - jnp.take(cs, lo, axis=1)
    cnt = (hi - lo).astype(jnp.float32)
    return (s / cnt[None, :, None] - u.astype(jnp.float32)).astype(u.dtype)


def pool_mixer(u, w_pool, pool_scale):
    B, L, _ = u.shape
    ug = u.reshape(B, L, len(POOL_WINDOWS), POOL_GROUP)
    pooled = jnp.stack([centred_pool_minus_self(ug[:, :, i], w) for i, w in enumerate(POOL_WINDOWS)], axis=2)
    y = jnp.einsum('blgc,gcd->blgd', pooled, w_pool)
    return y.reshape(B, L, POOL_WIDTH) * pool_scale


def depthwise_conv3(u, w):
    return lax.conv_general_dilated(u, w[:, None, :], window_strides=(1,),
                                    padding=((CONV_K // 2, CONV_K // 2),),
                                    dimension_numbers=('NWC', 'WIO', 'NWC'),
                                    feature_group_count=u.shape[-1])


def attend_blocks(q, k, v):
    B, L = q.shape[:2]
    G = N_Q_HEADS // N_KV_HEADS
    nb = L // Q_BLOCK
    qb = q.reshape(B, nb, Q_BLOCK, N_KV_HEADS, G, HEAD_DIM).transpose(1, 0, 2, 3, 4, 5)
    scale = HEAD_DIM ** -0.5

    def block(qblk):
        s = jnp.einsum('bqhgd,bkhd->bhgqk', qblk, k, preferred_element_type=jnp.float32) * scale
        p = jax.nn.softmax(s, axis=-1).astype(v.dtype)
        return jnp.einsum('bhgqk,bkhd->bqhgd', p, v)

    o = lax.map(block, qb)
    return o.transpose(1, 0, 2, 3, 4, 5).reshape(B, L, ATTN_WIDTH)


def kv_heads(k, v, g_k):
    B, L = k.shape[:2]
    k = rmsnorm(k.reshape(B, L, N_KV_HEADS, HEAD_DIM), g_k)
    return k, v.reshape(B, L, N_KV_HEADS, HEAD_DIM)


def mix_stream(p, q, k_all, v_all, w_pool, pool_scale, conv_w, w_out):
    y_pool = pool_mixer(p[..., :Q_OFF], w_pool, pool_scale)
    y_attn = attend_blocks(q, k_all, v_all)
    gate_b, gate_c, xin = p[..., CB_OFF:CC_OFF], p[..., CC_OFF:CX_OFF], p[..., CX_OFF:]
    y_conv = gate_b * depthwise_conv3(gate_c * xin, conv_w)
    return jnp.concatenate([y_pool, y_attn, y_conv], axis=-1) @ w_out


def swiglu(h, wg, wu, wd):
    return (jax.nn.silu(h @ wg) * (h @ wu)) @ wd


def moe_swiglu(h, w_router, b_router, wg, wu, wd):
    logits = (h @ w_router).astype(jnp.float32) + b_router.astype(jnp.float32)
    top_v, top_i = lax.top_k(logits, TOP_K)
    top_w = jax.nn.softmax(top_v, axis=-1)
    gates = jnp.sum(jax.nn.one_hot(top_i, N_EXPERTS, dtype=jnp.float32) * top_w[..., None], axis=-2).astype(h.dtype)
    out = jnp.zeros_like(h)
    for e in range(N_EXPERTS):
        out = out + gates[..., e:e + 1] * swiglu(h, wg[e], wu[e], wd[e])
    return out


def setup_inputs(seed: int = 0) -> dict:
    key = jax.random.key(seed)
    ks = iter(jax.random.split(key, 32))

    def nrm(shape, scale):
        return jax.random.normal(next(ks), shape, jnp.float32) * scale

    D = D_MODEL
    return {
        "x": nrm((BATCH, SEQ, D), 1.0),
        "c": nrm((BATCH, D), 1.0),
        "ctx": nrm((BATCH, CTX_LEN, D), 1.0),
        "c_ctx": nrm((D,), 1.0),
        "w_mod": nrm((DEPTH, D, N_MOD * D), 0.5 * D ** -0.5),
        "b_mod": nrm((DEPTH, N_MOD * D), 0.02),
        "g_mix": 1.0 + nrm((DEPTH, D), 0.05),
        "w_in": nrm((DEPTH, D, IN_WIDTH), D ** -0.5),
        "w_pool": nrm((DEPTH, len(POOL_WINDOWS), POOL_GROUP, POOL_GROUP), POOL_GROUP ** -0.5),
        "pool_scale": 1.0 + nrm((DEPTH, POOL_WIDTH), 0.1),
        "g_q": 1.0 + nrm((DEPTH, HEAD_DIM), 0.05),
        "g_k": 1.0 + nrm((DEPTH, HEAD_DIM), 0.05),
        "conv_w": nrm((DEPTH, CONV_K, CONV_WIDTH), CONV_K ** -0.5),
        "w_out": nrm((DEPTH, MIX_WIDTH, D), MIX_WIDTH ** -0.5),
        "g_ffn": 1.0 + nrm((DEPTH, D), 0.05),
        "w_gate_dense": nrm((N_DENSE, D, D_FF_DENSE), D ** -0.5),
        "w_up_dense": nrm((N_DENSE, D, D_FF_DENSE), D ** -0.5),
        "w_down_dense": nrm((N_DENSE, D_FF_DENSE, D), D_FF_DENSE ** -0.5),
        "w_router": nrm((N_MOE, D, N_EXPERTS), D ** -0.5),
        "b_router": nrm((N_MOE, N_EXPERTS), 0.01),
        "w_gate_exp": nrm((N_MOE, N_EXPERTS, D, D_FF_EXPERT), D ** -0.5),
        "w_up_exp": nrm((N_MOE, N_EXPERTS, D, D_FF_EXPERT), D ** -0.5),
        "w_down_exp": nrm((N_MOE, N_EXPERTS, D_FF_EXPERT, D), D_FF_EXPERT ** -0.5),
        "g_final": 1.0 + nrm((D,), 0.05),
    }


def reference(x, c, ctx, c_ctx, w_mod, b_mod, g_mix, w_in, w_pool, pool_scale, g_q, g_k, conv_w,
              w_out, g_ffn, w_gate_dense, w_up_dense, w_down_dense, w_router, b_router,
              w_gate_exp, w_up_exp, w_down_exp, g_final):
    B, L, _ = x.shape
    cos, sin = rope_tables(L)
    for l in range(DEPTH):
        last = l == DEPTH - 1
        sa_x, ca_x, ga_x, sf_x, cf_x, gf_x = layer_mods(c, w_mod[l], b_mod[l])
        sa_c, ca_c, ga_c, sf_c, cf_c, gf_c = layer_mods(c_ctx[None], w_mod[l], b_mod[l])

        hc = modulate(rmsnorm(ctx, g_mix[l]), sa_c, ca_c)
        if last:
            pkv = hc @ w_in[l][:, K_OFF:CB_OFF]
            kc, vc = kv_heads(pkv[..., :KV_WIDTH], pkv[..., KV_WIDTH:], g_k[l])
        else:
            pc = hc @ w_in[l]
            kc, vc = kv_heads(pc[..., K_OFF:V_OFF], pc[..., V_OFF:CB_OFF], g_k[l])
            qc = rmsnorm(pc[..., Q_OFF:K_OFF].reshape(B, CTX_LEN, N_Q_HEADS, HEAD_DIM), g_q[l])
            ctx_mix = mix_stream(pc, qc, kc, vc, w_pool[l], pool_scale[l], conv_w[l], w_out[l])

        hx = modulate(rmsnorm(x, g_mix[l]), sa_x, ca_x)
        px = hx @ w_in[l]
        kx, vx = kv_heads(px[..., K_OFF:V_OFF], px[..., V_OFF:CB_OFF], g_k[l])
        kx = apply_rope(kx, cos, sin)
        qx = rmsnorm(px[..., Q_OFF:K_OFF].reshape(B, L, N_Q_HEADS, HEAD_DIM), g_q[l])
        qx = apply_rope(qx, cos, sin)
        k_all = jnp.concatenate([kc, kx], axis=1)
        v_all = jnp.concatenate([vc, vx], axis=1)
        x = x + ga_x[:, None, :] * mix_stream(px, qx, k_all, v_all, w_pool[l], pool_scale[l], conv_w[l], w_out[l])

        if l % 2 == 0:
            i = l // 2
            ffn = lambda h: swiglu(h, w_gate_dense[i], w_up_dense[i], w_down_dense[i])
        else:
            i = l // 2
            ffn = lambda h: moe_swiglu(h, w_router[i], b_router[i], w_gate_exp[i], w_up_exp[i], w_down_exp[i])
        x = x + gf_x[:, None, :] * ffn(modulate(rmsnorm(x, g_ffn[l]), sf_x, cf_x))
        if not last:
            ctx = ctx + ga_c[:, None, :] * ctx_mix
            ctx = ctx + gf_c[:, None, :] * ffn(modulate(rmsnorm(ctx, g_ffn[l]), sf_c, cf_c))
    return rmsnorm(x, g_final)
```

```python
import functools

import jax
import jax.numpy as jnp
from jax import lax
from jax.experimental import pallas as pl
from jax.experimental.pallas import tpu as pltpu

F32 = jnp.float32
BF16 = jnp.bfloat16

POOL_WINDOWS = (2, 4, 8, 16)
HEAD_DIM = 128
KV_GROUP = 4
N_MOD = 6
N_EXPERTS = 8
TOP_K = 2
GRID_W = 64
ROPE_BASE = 10000.0
EPS = 1e-6

LANES = 128
SUBLANES = 8
VMEM_LIMIT_BYTES = 56 * 1024 * 1024

POOL_HALO = 16
CONV_HALO = 8
NEG_BIG = -1e30


def _params(*sem):
    return pltpu.CompilerParams(dimension_semantics=sem, vmem_limit_bytes=VMEM_LIMIT_BYTES)


def _silu(x):
    return x * jax.nn.sigmoid(x)


def _mods_kernel(c_ref, w_ref, b_ref, o_ref):
    s = _silu(c_ref[...]).astype(BF16)
    w = w_ref[...].astype(BF16)
    o_ref[...] = jnp.dot(s, w, preferred_element_type=F32) + b_ref[...]


def _mods(cvec, w_mod, b_mod, tn=512):
    depth, d, n = w_mod.shape
    return pl.pallas_call(
        _mods_kernel,
        grid=(depth, n // tn),
        in_specs=[
            pl.BlockSpec((SUBLANES, d), lambda l, j: (0, 0)),
            pl.BlockSpec((None, d, tn), lambda l, j: (l, 0, j)),
            pl.BlockSpec((None, 1, tn), lambda l, j: (l, 0, j)),
        ],
        out_specs=pl.BlockSpec((None, SUBLANES, tn), lambda l, j: (l, 0, j)),
        out_shape=jax.ShapeDtypeStruct((depth, SUBLANES, n), F32),
        compiler_params=_params("parallel", "parallel"),
        name="mods",
    )(cvec, w_mod, b_mod.reshape(depth, 1, n))


def _rms(x, g):
    return (x * lax.rsqrt(jnp.mean(x * x, axis=-1, keepdims=True) + EPS)) * g


def _norm_kernel(*refs, has_resid, mode):
    it = iter(refs)
    x_ref = next(it)
    x = x_ref[...]
    if has_resid:
        y_ref, gate_ref = next(it), next(it)
        x = x + gate_ref[...] * y_ref[...].astype(F32)
    g_ref = next(it)
    if mode == "final":
        o_ref = next(it)
        o_ref[...] = _rms(x, g_ref[...])
        return
    shift_ref, scale_ref = next(it), next(it)
    if mode == "route":
        wr_ref, br_ref = next(it), next(it)
    if has_resid:
        xo_ref = next(it)
        xo_ref[...] = x
    h_ref = next(it)
    h = _rms(x, g_ref[...]) * (1.0 + scale_ref[...]) + shift_ref[...]
    h_ref[...] = h.astype(BF16)
    if mode == "route":
        idx_ref, gw_ref = next(it), next(it)
        logits = jnp.dot(h, wr_ref[...], preferred_element_type=F32,
                         precision=lax.Precision.HIGHEST) + br_ref[...]
        lane = lax.broadcasted_iota(jnp.int32, logits.shape, 1).astype(F32)
        m1 = jnp.max(logits, axis=-1, keepdims=True)
        i1 = jnp.min(jnp.where(logits == m1, lane, float(LANES)), axis=-1, keepdims=True)
        rest = jnp.where(lane == i1, -jnp.inf, logits)
        m2 = jnp.max(rest, axis=-1, keepdims=True)
        i2 = jnp.min(jnp.where(rest == m2, lane, float(LANES)), axis=-1, keepdims=True)
        e = jnp.exp(m2 - m1)
        den = 1.0 + e
        idx_ref[...] = jnp.where(lane == 0.0, i1, jnp.where(lane == 1.0, i2, 0.0)).astype(jnp.int32)
        gw_ref[...] = jnp.where(lane == 0.0, 1.0 / den, jnp.where(lane == 1.0, e / den, 0.0))


def _norm(x, g, shift=None, scale=None, resid=None, router=None, mode="mod", tm=256):
    m, d = x.shape
    tm = min(tm, m)
    row = pl.BlockSpec((tm, d), lambda i: (i, 0))
    vec = pl.BlockSpec((1, d), lambda i: (0, 0))
    lane_blk = pl.BlockSpec((tm, LANES), lambda i: (i, 0))
    args, in_specs = [x], [row]
    if resid is not None:
        args += [resid[0], resid[1]]
        in_specs += [row, vec]
    args.append(g)
    in_specs.append(vec)
    out_shape, out_specs = [], []
    if mode != "final":
        args += [shift, scale]
        in_specs += [vec, vec]
        if mode == "route":
            args += [router[0], router[1]]
            in_specs += [pl.BlockSpec((d, LANES), lambda i: (0, 0)), pl.BlockSpec((1, LANES), lambda i: (0, 0))]
        if resid is not None:
            out_shape.append(jax.ShapeDtypeStruct((m, d), F32))
            out_specs.append(row)
        out_shape.append(jax.ShapeDtypeStruct((m, d), BF16))
        out_specs.append(row)
        if mode == "route":
            out_shape += [jax.ShapeDtypeStruct((m, LANES), jnp.int32), jax.ShapeDtypeStruct((m, LANES), F32)]
            out_specs += [lane_blk, lane_blk]
    else:
        out_shape.append(jax.ShapeDtypeStruct((m, d), F32))
        out_specs.append(row)
    return pl.pallas_call(
        functools.partial(_norm_kernel, has_resid=resid is not None, mode=mode),
        grid=(m // tm,),
        in_specs=in_specs,
        out_specs=out_specs,
        out_shape=out_shape,
        compiler_params=_params("parallel"),
        name="norm_" + mode,
    )(*args)


def _proj_kernel(a_ref, w_ref, o_ref):
    o_ref[...] = jnp.dot(a_ref[...], w_ref[...].astype(BF16), preferred_element_type=F32).astype(o_ref.dtype)


def _proj(a, w, layer, n_out, col0=0, tm=1024, tn=512, out_dtype=F32):
    m, k = a.shape
    tm = min(tm, m)
    cb = col0 // tn
    return pl.pallas_call(
        _proj_kernel,
        grid=(m // tm, n_out // tn),
        in_specs=[
            pl.BlockSpec((tm, k), lambda i, j: (i, 0)),
            pl.BlockSpec((None, k, tn), lambda i, j: (layer, 0, j + cb)),
        ],
        out_specs=pl.BlockSpec((tm, tn), lambda i, j: (i, j)),
        out_shape=jax.ShapeDtypeStruct((m, n_out), out_dtype),
        compiler_params=_params("parallel", "arbitrary"),
        name="proj",
    )(a, w)


def _gate_up_kernel(a_ref, wg_ref, wu_ref, o_ref):
    a = a_ref[...]
    g = jnp.dot(a, wg_ref[...].astype(BF16), preferred_element_type=F32)
    u = jnp.dot(a, wu_ref[...].astype(BF16), preferred_element_type=F32)
    o_ref[...] = (_silu(g) * u).astype(o_ref.dtype)


def _gate_up(a, wg, wu, layer, tm=1024, tn=256):
    m, k = a.shape
    n = wg.shape[-1]
    tm = min(tm, m)
    wspec = pl.BlockSpec((None, k, tn), lambda i, j: (layer, 0, j))
    return pl.pallas_call(
        _gate_up_kernel,
        grid=(m // tm, n // tn),
        in_specs=[pl.BlockSpec((tm, k), lambda i, j: (i, 0)), wspec, wspec],
        out_specs=pl.BlockSpec((tm, tn), lambda i, j: (i, j)),
        out_shape=jax.ShapeDtypeStruct((m, n), BF16),
        compiler_params=_params("parallel", "arbitrary"),
        name="gate_up",
    )(a, wg, wu)


def _qkv_kernel(*refs, want_q):
    if want_q:
        q0_ref, q1_ref, kv_ref, cos_ref, sin_ref, gq_ref, gk_ref, q_ref, k_ref, v_ref = refs
    else:
        kv_ref, cos_ref, sin_ref, gk_ref, k_ref, v_ref = refs
    cos, sin = cos_ref[...], sin_ref[...]

    def head(x, g):
        y = _rms(x, g)
        return y * cos + pltpu.roll(y, HEAD_DIM // 2, axis=1) * sin

    if want_q:
        heads_per_blk = q0_ref.shape[1] // HEAD_DIM
        for b, src in enumerate((q0_ref, q1_ref)):
            for hh in range(heads_per_blk):
                lo = hh * HEAD_DIM
                dst = (b * heads_per_blk + hh) * HEAD_DIM
                q_ref[:, dst:dst + HEAD_DIM] = head(src[:, lo:lo + HEAD_DIM], gq_ref[...]).astype(BF16)
    kvw = k_ref.shape[1]
    for hh in range(kvw // HEAD_DIM):
        lo = hh * HEAD_DIM
        k_ref[:, lo:lo + HEAD_DIM] = head(kv_ref[:, lo:lo + HEAD_DIM], gk_ref[...]).astype(BF16)
    v_ref[...] = kv_ref[:, kvw:].astype(BF16)


def _qkv_prep(p, cosf, sinf, gq_scaled, gk, q_col0, kv_col0, tm=256):
    m = p.shape[0]
    tm = min(tm, m)
    blk = 2 * KV_GROUP * HEAD_DIM
    kvw = blk // 2
    want_q = q_col0 is not None
    pspec = lambda c: pl.BlockSpec((tm, blk), lambda i: (i, c))
    tab = pl.BlockSpec((tm, HEAD_DIM), lambda i: (i, 0))
    gvec = pl.BlockSpec((1, HEAD_DIM), lambda i: (0, 0))
    kv_out = pl.BlockSpec((tm, kvw), lambda i: (i, 0))
    kv_shape = jax.ShapeDtypeStruct((m, kvw), BF16)
    if want_q:
        args = (p, p, p, cosf, sinf, gq_scaled, gk)
        in_specs = [pspec(q_col0 // blk), pspec(q_col0 // blk + 1), pspec(kv_col0 // blk), tab, tab, gvec, gvec]
        out_specs = [pl.BlockSpec((tm, 2 * blk), lambda i: (i, 0)), kv_out, kv_out]
        out_shape = [jax.ShapeDtypeStruct((m, 2 * blk), BF16), kv_shape, kv_shape]
    else:
        args = (p, cosf, sinf, gk)
        in_specs = [pspec(kv_col0 // blk), tab, tab, gvec]
        out_specs = [kv_out, kv_out]
        out_shape = [kv_shape, kv_shape]
    return pl.pallas_call(
        functools.partial(_qkv_kernel, want_q=want_q),
        grid=(m // tm,),
        in_specs=in_specs,
        out_specs=out_specs,
        out_shape=out_shape,
        compiler_params=_params("parallel"),
        name="qkv_prep",
    )(*args)


def _flash_kernel(q_ref, kt_ref, v_ref, o_ref, m_ref, l_ref, acc_ref):
    ki = pl.program_id(2)

    @pl.when(ki == 0)
    def _():
        m_ref[...] = jnp.full(m_ref.shape, -jnp.inf, F32)
        l_ref[...] = jnp.zeros(l_ref.shape, F32)
        acc_ref[...] = jnp.zeros(acc_ref.shape, F32)

    kt = kt_ref[...]
    v = v_ref[...]
    for g in range(KV_GROUP):
        lo = g * HEAD_DIM
        s = jnp.dot(q_ref[:, lo:lo + HEAD_DIM], kt, preferred_element_type=F32)
        m_prev = m_ref[g]
        m_new = jnp.maximum(m_prev, jnp.max(s, axis=-1, keepdims=True))
        alpha = jnp.exp(m_prev - m_new)
        p = jnp.exp(s - m_new)
        l_ref[g] = alpha * l_ref[g] + jnp.sum(p, axis=-1, keepdims=True)
        acc_ref[g] = alpha * acc_ref[g] + jnp.dot(p.astype(BF16), v, preferred_element_type=F32)
        m_ref[g] = m_new

    @pl.when(ki == pl.num_programs(2) - 1)
    def _():
        for g in range(KV_GROUP):
            lo = g * HEAD_DIM
            o_ref[:, lo:lo + HEAD_DIM] = (acc_ref[g] / l_ref[g]).astype(o_ref.dtype)


def _flash(q, kt, v, tq=256, tk=1408):
    lq, qw = q.shape
    lk = v.shape[0]
    hkv = v.shape[1] // HEAD_DIM
    tq, tk = min(tq, lq), min(tk, lk)
    gw = KV_GROUP * HEAD_DIM
    return pl.pallas_call(
        _flash_kernel,
        grid=(hkv, lq // tq, lk // tk),
        in_specs=[
            pl.BlockSpec((tq, gw), lambda h, i, j: (i, h)),
            pl.BlockSpec((HEAD_DIM, tk), lambda h, i, j: (h, j)),
            pl.BlockSpec((tk, HEAD_DIM), lambda h, i, j: (j, h)),
        ],
        out_specs=pl.BlockSpec((tq, gw), lambda h, i, j: (i, h)),
        out_shape=jax.ShapeDtypeStruct((lq, qw), BF16),
        scratch_shapes=[pltpu.VMEM((KV_GROUP, tq, 1), F32),
                        pltpu.VMEM((KV_GROUP, tq, 1), F32),
                        pltpu.VMEM((KV_GROUP, tq, HEAD_DIM), F32)],
        compiler_params=_params("parallel", "parallel", "arbitrary"),
        name="flash",
    )(q, kt, v)


def _mixer_kernel(pc_ref, pp_ref, pn_ref, cb_ref, cc_ref, ccp_ref, ccn_ref, cx_ref, cxp_ref, cxn_ref,
                  ya_ref, wp_ref, ps_ref, cw_ref, o_ref, *, seq_len):
    i = pl.program_id(0)
    last = pl.num_programs(0) - 1
    tm = pc_ref.shape[0]
    n_ext = tm + 2 * POOL_HALO
    pool_w = pc_ref.shape[1]
    grp = pool_w // len(POOL_WINDOWS)

    prev = jnp.where(i == 0, 0.0, pp_ref[...])
    nxt = jnp.where(i == last, 0.0, pn_ref[...])
    ext = jnp.concatenate([prev, pc_ref[...], nxt], axis=0)
    t = (i * tm + lax.broadcasted_iota(jnp.int32, (tm, 1), 0))

    def shift_up(x, k):
        return pltpu.roll(x, n_ext - k, axis=0)

    def shift_down(x, k):
        return pltpu.roll(x, k, axis=0)

    for gi, w in enumerate(POOL_WINDOWS):
        e = ext[:, gi * grp:(gi + 1) * grp]
        half = w // 2
        part, span = e, 1
        while span < half:
            part = part + shift_up(part, span)
            span *= 2
        win = shift_down(part, half) + part
        win = win[POOL_HALO:POOL_HALO + tm]
        lo = jnp.clip(t - half, 0, seq_len)
        hi = jnp.clip(t - half + w, 0, seq_len)
        cnt = (hi - lo).astype(F32)
        pooled = win / cnt - e[POOL_HALO:POOL_HALO + tm]
        y = jnp.dot(pooled.astype(BF16), wp_ref[gi].astype(BF16), preferred_element_type=F32)
        o_ref[:, gi * grp:(gi + 1) * grp] = (y * ps_ref[:, gi * grp:(gi + 1) * grp]).astype(o_ref.dtype)

    aw = ya_ref.shape[1]
    o_ref[:, pool_w:pool_w + aw] = ya_ref[...]

    u = cc_ref[...] * cx_ref[...]
    u_before = jnp.where(i == 0, 0.0, ccp_ref[CONV_HALO - 1:CONV_HALO, :] * cxp_ref[CONV_HALO - 1:CONV_HALO, :])
    u_after = jnp.where(i == last, 0.0, ccn_ref[0:1, :] * cxn_ref[0:1, :])
    r = lax.broadcasted_iota(jnp.int32, (tm, 1), 0)
    um1 = jnp.where(r == 0, u_before, pltpu.roll(u, 1, axis=0))
    up1 = jnp.where(r == tm - 1, u_after, pltpu.roll(u, tm - 1, axis=0))
    conv = cw_ref[0:1, :] * um1 + cw_ref[1:2, :] * u + cw_ref[2:3, :] * up1
    o_ref[:, pool_w + aw:] = (cb_ref[...] * conv).astype(o_ref.dtype)


def _mixer(p, ya, w_pool, pool_scale, conv_w, cb_col0, tm=256):
    m = p.shape[0]
    tm = min(tm, m)
    blk = pool_scale.shape[-1]
    c0 = cb_col0 // blk
    nblk_pool, nblk_conv = m // POOL_HALO, m // CONV_HALO
    rp, rc = tm // POOL_HALO, tm // CONV_HALO

    cur = lambda c: pl.BlockSpec((tm, blk), lambda i: (i, c))
    prev_p = pl.BlockSpec((POOL_HALO, blk), lambda i: (jnp.maximum(i * rp - 1, 0), 0))
    next_p = pl.BlockSpec((POOL_HALO, blk), lambda i: (jnp.minimum((i + 1) * rp, nblk_pool - 1), 0))
    prev_c = lambda c: pl.BlockSpec((CONV_HALO, blk), lambda i: (jnp.maximum(i * rc - 1, 0), c))
    next_c = lambda c: pl.BlockSpec((CONV_HALO, blk), lambda i: (jnp.minimum((i + 1) * rc, nblk_conv - 1), c))
    aw = ya.shape[1]
    full = lambda a: pl.BlockSpec(a.shape, lambda i: (0,) * a.ndim)
    ps = pool_scale.reshape(1, blk)
    return pl.pallas_call(
        functools.partial(_mixer_kernel, seq_len=m),
        grid=(m // tm,),
        in_specs=[cur(0), prev_p, next_p,
                  cur(c0), cur(c0 + 1), prev_c(c0 + 1), next_c(c0 + 1),
                  cur(c0 + 2), prev_c(c0 + 2), next_c(c0 + 2),
                  pl.BlockSpec((tm, aw), lambda i: (i, 0)),
                  full(w_pool), full(ps), full(conv_w)],
        out_specs=pl.BlockSpec((tm, 2 * blk + aw), lambda i: (i, 0)),
        out_shape=jax.ShapeDtypeStruct((m, 2 * blk + aw), BF16),
        compiler_params=_params("parallel"),
        name="mixer",
    )(p, p, p, p, p, p, p, p, p, p, ya, w_pool, ps, conv_w)


def _down(act, wd, layer, tm=512, tn=256):
    return _proj(act, wd, layer, wd.shape[-1], tm=tm, tn=tn)


def _moe_gate_up_kernel(te_ref, nv_ref, a_ref, wg_ref, wu_ref, o_ref):
    @pl.when(pl.program_id(1) < nv_ref[0])
    def _():
        _gate_up_kernel(a_ref, wg_ref, wu_ref, o_ref)


def _moe_down_kernel(te_ref, nv_ref, a_ref, w_ref, o_ref):
    @pl.when(pl.program_id(1) < nv_ref[0])
    def _():
        _proj_kernel(a_ref, w_ref, o_ref)


def _moe_call(kernel, a, weights, tile_expert, n_valid, n_out, out_dtype, tm, tn, name):
    r, k = a.shape
    n_tiles = r // tm
    row = lambda j, t, te, nv: (jnp.minimum(t, nv[0] - 1), 0)
    wspec = pl.BlockSpec((None, k, tn), lambda j, t, te, nv: (te[t], 0, j))
    return pl.pallas_call(
        kernel,
        grid_spec=pltpu.PrefetchScalarGridSpec(
            num_scalar_prefetch=2,
            grid=(n_out // tn, n_tiles),
            in_specs=[pl.BlockSpec((tm, k), row)] + [wspec] * len(weights),
            out_specs=pl.BlockSpec((tm, tn), lambda j, t, te, nv: (jnp.minimum(t, nv[0] - 1), j)),
        ),
        out_shape=jax.ShapeDtypeStruct((r, n_out), out_dtype),
        compiler_params=_params("parallel", "arbitrary"),
        name=name,
    )(tile_expert, n_valid, a, *weights)


def _moe_ffn(h, ridx, wg, wu, wd, tm=512, tn=512):
    m, d = h.shape
    n_slots = m * TOP_K
    r_max = n_slots + N_EXPERTS * tm
    n_tiles = r_max // tm
    e_flat = ridx[:, :TOP_K].reshape(n_slots)
    order = jnp.argsort(e_flat, stable=True).astype(jnp.int32)
    counts = jnp.sum((e_flat[:, None] == jnp.arange(N_EXPERTS)[None, :]).astype(jnp.int32), axis=0)
    padded = ((counts + tm - 1) // tm) * tm
    gend = jnp.cumsum(padded)
    gstart = gend - padded
    cstart = jnp.cumsum(counts) - counts
    n_valid = (gend[-1] // tm).astype(jnp.int32).reshape(1)
    tile_expert = jnp.minimum(jnp.searchsorted(gend, jnp.arange(n_tiles) * tm, side="right"),
                              N_EXPERTS - 1).astype(jnp.int32)
    last_e = tile_expert[jnp.maximum(n_valid[0] - 1, 0)]
    tile_expert = jnp.where(jnp.arange(n_tiles) < n_valid[0], tile_expert, last_e)
    rows = jnp.arange(r_max)
    row_e = tile_expert[rows // tm]
    off = rows - gstart[row_e]
    row_valid = (off < counts[row_e]) & (rows < gend[-1])
    src = jnp.clip(cstart[row_e] + off, 0, n_slots - 1)
    row_token = jnp.where(row_valid, order[src] // TOP_K, 0)
    rank = jnp.argsort(order).astype(jnp.int32)
    slot_row = (gstart[e_flat] + rank - cstart[e_flat]).reshape(m, TOP_K)

    hs = jnp.take(h, row_token, axis=0)
    act = _moe_call(_moe_gate_up_kernel, hs, (wg, wu), tile_expert, n_valid, wg.shape[-1], BF16, tm, tn,
                    "moe_gate_up")
    ys = _moe_call(_moe_down_kernel, act, (wd,), tile_expert, n_valid, d, F32, tm, tn, "moe_down")
    return ys, slot_row


def _combine_kernel(x_ref, y1_ref, y2_ref, gw_ref, gate_ref, g_ref, o_ref):
    gw = gw_ref[...]
    y = gw[:, 0:1] * y1_ref[...] + gw[:, 1:2] * y2_ref[...]
    o_ref[...] = _rms(x_ref[...] + gate_ref[...] * y, g_ref[...])


def _combine_final(x, y1, y2, gw, gate, g, tm=256):
    m, d = x.shape
    row = pl.BlockSpec((tm, d), lambda i: (i, 0))
    vec = pl.BlockSpec((1, d), lambda i: (0, 0))
    return pl.pallas_call(
        _combine_kernel,
        grid=(m // tm,),
        in_specs=[row, row, row, pl.BlockSpec((tm, LANES), lambda i: (i, 0)), vec, vec],
        out_specs=row,
        out_shape=jax.ShapeDtypeStruct((m, d), F32),
        compiler_params=_params("parallel"),
        name="combine_final",
    )(x, y1, y2, gw, gate, g)


def _rope_tables(seq_len):
    rows = seq_len // GRID_W
    row = jnp.repeat(jnp.arange(rows), GRID_W).astype(F32)
    col = jnp.tile(jnp.arange(GRID_W), rows).astype(F32)
    n_axis = HEAD_DIM // 4
    inv = ROPE_BASE ** (-jnp.arange(n_axis, dtype=F32) / n_axis)
    ang = jnp.concatenate([row[:, None] * inv, col[:, None] * inv], axis=-1)
    cos, sin = jnp.cos(ang), jnp.sin(ang)
    return jnp.concatenate([cos, cos], axis=-1), jnp.concatenate([-sin, sin], axis=-1)


def kernel(x, c, ctx, c_ctx, w_mod, b_mod, g_mix, w_in, w_pool, pool_scale, g_q, g_k, conv_w, w_out, g_ffn,
           w_gate_dense, w_up_dense, w_down_dense, w_router, b_router, w_gate_exp, w_up_exp, w_down_exp, g_final):
    batch, seq, d = x.shape
    depth = w_mod.shape[0]
    assert batch == 1 and depth == 2
    n_ctx = ctx.shape[1]
    pool_w = pool_scale.shape[-1]
    conv_wd = conv_w.shape[-1]
    q_w = (d // (2 * HEAD_DIM)) * HEAD_DIM
    kv_w = q_w // KV_GROUP
    q_off = pool_w
    k_off = q_off + q_w
    cb_off = k_off + 2 * kv_w
    in_w = cb_off + 3 * conv_wd

    xs, cs = x[0], ctx[0]
    cvec = jnp.zeros((SUBLANES, d), F32).at[0].set(c[0]).at[1].set(c_ctx)
    mods = _mods(cvec, w_mod, b_mod)

    def mod(l, row, k):
        return mods[l, row:row + 1, k * d:(k + 1) * d]

    cosf, sinf = _rope_tables(seq)
    cos_c, sin_c = jnp.ones((n_ctx, HEAD_DIM), F32), jnp.zeros((n_ctx, HEAD_DIM), F32)
    qscale = HEAD_DIM ** -0.5

    pend_x = pend_c = None
    for l in range(depth):
        last = l == depth - 1
        vec = lambda a: a[l].reshape(1, -1)
        gq, gk = vec(g_q) * qscale, vec(g_k)

        def first_norm(stream, pend, row):
            if pend is None:
                (h,) = _norm(stream, vec(g_mix), mod(l, row, 0), mod(l, row, 1))
                return stream, h
            return _norm(stream, vec(g_mix), mod(l, row, 0), mod(l, row, 1), resid=pend)

        cs, hc = first_norm(cs, pend_c, 1)
        if last:
            pkv = _proj(hc, w_in, l, 2 * kv_w, col0=k_off)
            kc, vc = _qkv_prep(pkv, cos_c, sin_c, gq, gk, None, 0)
        else:
            pc = _proj(hc, w_in, l, in_w)
            qc, kc, vc = _qkv_prep(pc, cos_c, sin_c, gq, gk, q_off, k_off)
            yac = _flash(qc, kc.T, vc)
            mixc = _mixer(pc, yac, w_pool[l], pool_scale[l], conv_w[l], cb_off)
            ymc = _proj(mixc, w_out, l, d)

        xs, hx = first_norm(xs, pend_x, 0)
        px = _proj(hx, w_in, l, in_w)
        qx, kx, vx = _qkv_prep(px, cosf, sinf, gq, gk, q_off, k_off)
        k_all = jnp.concatenate([kc, kx], axis=0)
        v_all = jnp.concatenate([vc, vx], axis=0)
        yax = _flash(qx, k_all.T, v_all)
        mixx = _mixer(px, yax, w_pool[l], pool_scale[l], conv_w[l], cb_off)
        ymx = _proj(mixx, w_out, l, d)

        i = l // 2
        if l % 2 == 0:
            def ffn(stream, ym, row):
                xn, h = _norm(stream, vec(g_ffn), mod(l, row, 3), mod(l, row, 4), resid=(ym, mod(l, row, 2)))
                act = _gate_up(h, w_gate_dense, w_up_dense, i)
                return xn, _down(act, w_down_dense, i)

            xs, yfx = ffn(xs, ymx, 0)
            pend_x = (yfx, mod(l, 0, 5))
            if not last:
                cs, yfc = ffn(cs, ymc, 1)
                pend_c = (yfc, mod(l, 1, 5))
        else:
            wr = jnp.zeros((d, LANES), F32).at[:, :N_EXPERTS].set(w_router[i])
            br = jnp.full((1, LANES), NEG_BIG, F32).at[0, :N_EXPERTS].set(b_router[i])
            xs, hf, ridx, rgw = _norm(xs, vec(g_ffn), mod(l, 0, 3), mod(l, 0, 4), resid=(ymx, mod(l, 0, 2)),
                                      router=(wr, br), mode="route")
            ys, slot_row = _moe_ffn(hf, ridx, w_gate_exp[i], w_up_exp[i], w_down_exp[i])
            y1 = jnp.take(ys, slot_row[:, 0], axis=0)
            y2 = jnp.take(ys, slot_row[:, 1], axis=0)
            pend_x = (y1, y2, rgw, mod(l, 0, 5))

    if len(pend_x) == 4:
        out = _combine_final(xs, pend_x[0], pend_x[1], pend_x[2], pend_x[3], g_final.reshape(1, -1))
    else:
        (out,) = _norm(xs, g_final.reshape(1, -1), resid=pend_x, mode="final")
    return out[None]
```

```python
import functools

import jax
import jax.numpy as jnp
from jax import lax
from jax.experimental import pallas as pl
from jax.experimental.pallas import tpu as pltpu

F32 = jnp.float32
BF16 = jnp.bfloat16

POOL_WINDOWS = (2, 4, 8, 16)
HEAD_DIM = 128
KV_GROUP = 4
N_MOD = 6
N_EXPERTS = 8
TOP_K = 2
GRID_W = 64
ROPE_BASE = 10000.0
EPS = 1e-6

LANES = 128
SUBLANES = 8
VMEM_LIMIT_BYTES = 56 * 1024 * 1024

POOL_HALO = 16
CONV_HALO = 8
NEG_BIG = -1e30
LOG2E = 1.4426950408889634


def _params(*sem):
    return pltpu.CompilerParams(dimension_semantics=sem, vmem_limit_bytes=VMEM_LIMIT_BYTES)


def _silu(x):
    return x * jax.nn.sigmoid(x)


def _mods_kernel(c_ref, w_ref, b_ref, o_ref):
    s = _silu(c_ref[...]).astype(BF16)
    w = w_ref[...].astype(BF16)
    o_ref[...] = jnp.dot(s, w, preferred_element_type=F32) + b_ref[...]


def _mods(cvec, w_mod, b_mod, tn=512):
    depth, d, n = w_mod.shape
    return pl.pallas_call(
        _mods_kernel,
        grid=(depth, n // tn),
        in_specs=[
            pl.BlockSpec((SUBLANES, d), lambda l, j: (0, 0)),
            pl.BlockSpec((None, d, tn), lambda l, j: (l, 0, j)),
            pl.BlockSpec((None, 1, tn), lambda l, j: (l, 0, j)),
        ],
        out_specs=pl.BlockSpec((None, SUBLANES, tn), lambda l, j: (l, 0, j)),
        out_shape=jax.ShapeDtypeStruct((depth, SUBLANES, n), F32),
        compiler_params=_params("parallel", "parallel"),
        name="mods",
    )(cvec, w_mod, b_mod.reshape(depth, 1, n))


def _rms(x, g):
    return (x * lax.rsqrt(jnp.mean(x * x, axis=-1, keepdims=True) + EPS)) * g


def _norm_kernel(*refs, has_resid, mode):
    it = iter(refs)
    x_ref = next(it)
    x = x_ref[...]
    if has_resid:
        y_ref, gate_ref = next(it), next(it)
        x = x + gate_ref[...] * y_ref[...].astype(F32)
    g_ref = next(it)
    if mode == "final":
        o_ref = next(it)
        o_ref[...] = _rms(x, g_ref[...])
        return
    shift_ref, scale_ref = next(it), next(it)
    if mode == "route":
        wr_ref, br_ref = next(it), next(it)
    if has_resid:
        xo_ref = next(it)
        xo_ref[...] = x
    h_ref = next(it)
    h = _rms(x, g_ref[...]) * (1.0 + scale_ref[...]) + shift_ref[...]
    h_ref[...] = h.astype(BF16)
    if mode == "route":
        idx_ref, gw_ref = next(it), next(it)
        logits = jnp.dot(h, wr_ref[...], preferred_element_type=F32,
                         precision=lax.Precision.HIGHEST) + br_ref[...]
        lane = lax.broadcasted_iota(jnp.int32, logits.shape, 1).astype(F32)
        m1 = jnp.max(logits, axis=-1, keepdims=True)
        i1 = jnp.min(jnp.where(logits == m1, lane, float(LANES)), axis=-1, keepdims=True)
        rest = jnp.where(lane == i1, -jnp.inf, logits)
        m2 = jnp.max(rest, axis=-1, keepdims=True)
        i2 = jnp.min(jnp.where(rest == m2, lane, float(LANES)), axis=-1, keepdims=True)
        e = jnp.exp(m2 - m1)
        den = 1.0 + e
        idx_ref[...] = jnp.where(lane == 0.0, i1, jnp.where(lane == 1.0, i2, 0.0)).astype(jnp.int32)
        gw_ref[...] = jnp.where(lane == 0.0, 1.0 / den, jnp.where(lane == 1.0, e / den, 0.0))


def _norm(x, g, shift=None, scale=None, resid=None, router=None, mode="mod", tm=256):
    m, d = x.shape
    tm = min(tm, m)
    row = pl.BlockSpec((tm, d), lambda i: (i, 0))
    vec = pl.BlockSpec((1, d), lambda i: (0, 0))
    lane_blk = pl.BlockSpec((tm, LANES), lambda i: (i, 0))
    args, in_specs = [x], [row]
    if resid is not None:
        args += [resid[0], resid[1]]
        in_specs += [row, vec]
    args.append(g)
    in_specs.append(vec)
    out_shape, out_specs = [], []
    if mode != "final":
        args += [shift, scale]
        in_specs += [vec, vec]
        if mode == "route":
            args += [router[0], router[1]]
            in_specs += [pl.BlockSpec((d, LANES), lambda i: (0, 0)), pl.BlockSpec((1, LANES), lambda i: (0, 0))]
        if resid is not None:
            out_shape.append(jax.ShapeDtypeStruct((m, d), F32))
            out_specs.append(row)
        out_shape.append(jax.ShapeDtypeStruct((m, d), BF16))
        out_specs.append(row)
        if mode == "route":
            out_shape += [jax.ShapeDtypeStruct((m, LANES), jnp.int32), jax.ShapeDtypeStruct((m, LANES), F32)]
            out_specs += [lane_blk, lane_blk]
    else:
        out_shape.append(jax.ShapeDtypeStruct((m, d), F32))
        out_specs.append(row)
    return pl.pallas_call(
        functools.partial(_norm_kernel, has_resid=resid is not None, mode=mode),
        grid=(m // tm,),
        in_specs=in_specs,
        out_specs=out_specs,
        out_shape=out_shape,
        compiler_params=_params("parallel"),
        name="norm_" + mode,
    )(*args)


def _proj_kernel(a_ref, w_ref, o_ref):
    o_ref[...] = jnp.dot(a_ref[...], w_ref[...].astype(BF16), preferred_element_type=F32).astype(o_ref.dtype)


def _proj(a, w, layer, n_out, col0=0, tm=1024, tn=512, out_dtype=F32, a_buffers=2):
    m, k = a.shape
    tm = min(tm, m)
    cb = col0 // tn
    a_mode = {} if a_buffers == 2 else {"pipeline_mode": pl.Buffered(a_buffers)}
    return pl.pallas_call(
        _proj_kernel,
        grid=(m // tm, n_out // tn),
        in_specs=[
            pl.BlockSpec((tm, k), lambda i, j: (i, 0), **a_mode),
            pl.BlockSpec((None, k, tn), lambda i, j: (layer, 0, j + cb)),
        ],
        out_specs=pl.BlockSpec((tm, tn), lambda i, j: (i, j)),
        out_shape=jax.ShapeDtypeStruct((m, n_out), out_dtype),
        compiler_params=_params("parallel", "arbitrary"),
        name="proj",
    )(a, w)


def _gate_up_kernel(a_ref, wg_ref, wu_ref, o_ref):
    a = a_ref[...]
    g = jnp.dot(a, wg_ref[...].astype(BF16), preferred_element_type=F32)
    u = jnp.dot(a, wu_ref[...].astype(BF16), preferred_element_type=F32)
    o_ref[...] = (_silu(g) * u).astype(o_ref.dtype)


def _gate_up(a, wg, wu, layer, tm=1024, tn=256):
    m, k = a.shape
    n = wg.shape[-1]
    tm = min(tm, m)
    wspec = pl.BlockSpec((None, k, tn), lambda i, j: (layer, 0, j))
    return pl.pallas_call(
        _gate_up_kernel,
        grid=(m // tm, n // tn),
        in_specs=[pl.BlockSpec((tm, k), lambda i, j: (i, 0)), wspec, wspec],
        out_specs=pl.BlockSpec((tm, tn), lambda i, j: (i, j)),
        out_shape=jax.ShapeDtypeStruct((m, n), BF16),
        compiler_params=_params("parallel", "arbitrary"),
        name="gate_up",
    )(a, wg, wu)


def _qkv_kernel(*refs, want_q):
    if want_q:
        q0_ref, q1_ref, kv_ref, cos_ref, sin_ref, gq_ref, gk_ref, q_ref, k_ref, v_ref = refs
    else:
        kv_ref, cos_ref, sin_ref, gk_ref, k_ref, v_ref = refs
    cos, sin = cos_ref[...], sin_ref[...]

    def head(x, g):
        y = _rms(x, g)
        return y * cos + pltpu.roll(y, HEAD_DIM // 2, axis=1) * sin

    if want_q:
        heads_per_blk = q0_ref.shape[1] // HEAD_DIM
        for b, src in enumerate((q0_ref, q1_ref)):
            for hh in range(heads_per_blk):
                lo = hh * HEAD_DIM
                dst = (b * heads_per_blk + hh) * HEAD_DIM
                q_ref[:, dst:dst + HEAD_DIM] = head(src[:, lo:lo + HEAD_DIM], gq_ref[...]).astype(BF16)
    kvw = k_ref.shape[1]
    for hh in range(kvw // HEAD_DIM):
        lo = hh * HEAD_DIM
        k_ref[:, lo:lo + HEAD_DIM] = head(kv_ref[:, lo:lo + HEAD_DIM], gk_ref[...]).astype(BF16)
    v_ref[...] = kv_ref[:, kvw:].astype(BF16)


def _qkv_prep(p, cosf, sinf, gq_scaled, gk, q_col0, kv_col0, tm=256):
    m = p.shape[0]
    tm = min(tm, m)
    blk = 2 * KV_GROUP * HEAD_DIM
    kvw = blk // 2
    want_q = q_col0 is not None
    pspec = lambda c: pl.BlockSpec((tm, blk), lambda i: (i, c))
    tab = pl.BlockSpec((tm, HEAD_DIM), lambda i: (i, 0))
    gvec = pl.BlockSpec((1, HEAD_DIM), lambda i: (0, 0))
    kv_out = pl.BlockSpec((tm, kvw), lambda i: (i, 0))
    kv_shape = jax.ShapeDtypeStruct((m, kvw), BF16)
    if want_q:
        args = (p, p, p, cosf, sinf, gq_scaled, gk)
        in_specs = [pspec(q_col0 // blk), pspec(q_col0 // blk + 1), pspec(kv_col0 // blk), tab, tab, gvec, gvec]
        out_specs = [pl.BlockSpec((tm, 2 * blk), lambda i: (i, 0)), kv_out, kv_out]
        out_shape = [jax.ShapeDtypeStruct((m, 2 * blk), BF16), kv_shape, kv_shape]
    else:
        args = (p, cosf, sinf, gk)
        in_specs = [pspec(kv_col0 // blk), tab, tab, gvec]
        out_specs = [kv_out, kv_out]
        out_shape = [kv_shape, kv_shape]
    return pl.pallas_call(
        functools.partial(_qkv_kernel, want_q=want_q),
        grid=(m // tm,),
        in_specs=in_specs,
        out_specs=out_specs,
        out_shape=out_shape,
        compiler_params=_params("parallel"),
        name="qkv_prep",
    )(*args)


def _flash_t_kernel(qn_ref, kn_ref, qc_ref, kc_ref, vt_ref, o_ref,
                    s_a, s_b, mx_a, mx_b, m_ref, l_ref, acc_ref):
    h, i, j = pl.program_id(0), pl.program_id(1), pl.program_id(2)
    nq, nk = pl.num_programs(1), pl.num_programs(2)
    step = (h * nq + i) * nk + j

    def scores(k_ref, q_ref, s_ref, mx_ref):
        s = jnp.dot(k_ref[...], q_ref[...], preferred_element_type=F32)
        s_ref[...] = s
        mx_ref[...] = jnp.max(s, axis=0, keepdims=True)

    @pl.when(step == 0)
    def _():
        scores(kc_ref, qc_ref, s_a, mx_a)

    @pl.when(j == 0)
    def _():
        m_ref[...] = jnp.full(m_ref.shape, -jnp.inf, F32)
        l_ref[...] = jnp.zeros(l_ref.shape, F32)
        acc_ref[...] = jnp.zeros(acc_ref.shape, F32)

    def body(s_cur, mx_cur, s_nxt, mx_nxt):
        scores(kn_ref, qn_ref, s_nxt, mx_nxt)
        m_prev = m_ref[...]
        m_new = jnp.maximum(m_prev, mx_cur[...])
        alpha = jnp.exp2(m_prev - m_new)
        p = jnp.exp2(s_cur[...] - m_new)
        l_ref[...] = alpha * l_ref[...] + jnp.sum(p, axis=0, keepdims=True)
        acc_ref[...] = alpha * acc_ref[...] + jnp.dot(vt_ref[...], p.astype(BF16), preferred_element_type=F32)
        m_ref[...] = m_new

    @pl.when(step % 2 == 0)
    def _():
        body(s_a, mx_a, s_b, mx_b)

    @pl.when(step % 2 == 1)
    def _():
        body(s_b, mx_b, s_a, mx_a)

    @pl.when(j == nk - 1)
    def _():
        tq = o_ref.shape[0]
        o = acc_ref[...] / l_ref[...]
        for g in range(KV_GROUP):
            o_ref[:, g * HEAD_DIM:(g + 1) * HEAD_DIM] = o[:, g * tq:(g + 1) * tq].T.astype(o_ref.dtype)


def _flash_t(qt, k, vt, tk=1408):
    hkv, nq, _, nqc = qt.shape
    tq = nqc // KV_GROUP
    lk = k.shape[0]
    tk = min(tk, lk)
    nk = lk // tk
    gw = KV_GROUP * HEAD_DIM

    def nxt(h, i, j):
        j1 = j + 1
        i1 = i + j1 // nk
        h1 = h + i1 // nq
        return jnp.minimum(h1, hkv - 1), i1 % nq, j1 % nk

    def qn_map(h, i, j):
        h1, i1, _ = nxt(h, i, j)
        return h1, i1, 0, 0

    def kn_map(h, i, j):
        h1, _, j1 = nxt(h, i, j)
        return j1, h1

    qspec = lambda m: pl.BlockSpec((None, None, HEAD_DIM, nqc), m)
    return pl.pallas_call(
        _flash_t_kernel,
        grid=(hkv, nq, nk),
        in_specs=[
            qspec(qn_map),
            pl.BlockSpec((tk, HEAD_DIM), kn_map),
            qspec(lambda h, i, j: (h, i, 0, 0)),
            pl.BlockSpec((tk, HEAD_DIM), lambda h, i, j: (0, h)),
            pl.BlockSpec((HEAD_DIM, tk), lambda h, i, j: (h, j)),
        ],
        out_specs=pl.BlockSpec((tq, gw), lambda h, i, j: (i, h)),
        out_shape=jax.ShapeDtypeStruct((nq * tq, hkv * gw), BF16),
        scratch_shapes=[pltpu.VMEM((tk, nqc), F32), pltpu.VMEM((tk, nqc), F32),
                        pltpu.VMEM((1, nqc), F32), pltpu.VMEM((1, nqc), F32),
                        pltpu.VMEM((1, nqc), F32), pltpu.VMEM((1, nqc), F32),
                        pltpu.VMEM((HEAD_DIM, nqc), F32)],
        compiler_params=_params("arbitrary", "arbitrary", "arbitrary"),
        name="flash_t",
    )(qt, k, qt, k, vt)


def _heads_t(q, tq=512):
    l, w = q.shape
    tq = min(tq, l)
    hkv = w // (KV_GROUP * HEAD_DIM)
    q5 = q.reshape(l // tq, tq, hkv, KV_GROUP, HEAD_DIM)
    return q5.transpose(2, 0, 4, 3, 1).reshape(hkv, l // tq, HEAD_DIM, KV_GROUP * tq)


def _mixer_kernel(pc_ref, pp_ref, pn_ref, cb_ref, cc_ref, ccp_ref, ccn_ref, cx_ref, cxp_ref, cxn_ref,
                  ya_ref, wp_ref, ps_ref, cw_ref, o_ref, *, seq_len):
    i = pl.program_id(0)
    last = pl.num_programs(0) - 1
    tm = pc_ref.shape[0]
    n_ext = tm + 2 * POOL_HALO
    pool_w = pc_ref.shape[1]
    grp = pool_w // len(POOL_WINDOWS)

    prev = jnp.where(i == 0, 0.0, pp_ref[...])
    nxt = jnp.where(i == last, 0.0, pn_ref[...])
    ext = jnp.concatenate([prev, pc_ref[...], nxt], axis=0)
    t = (i * tm + lax.broadcasted_iota(jnp.int32, (tm, 1), 0))

    def shift_up(x, k):
        return pltpu.roll(x, n_ext - k, axis=0)

    def shift_down(x, k):
        return pltpu.roll(x, k, axis=0)

    for gi, w in enumerate(POOL_WINDOWS):
        e = ext[:, gi * grp:(gi + 1) * grp]
        half = w // 2
        part, span = e, 1
        while span < half:
            part = part + shift_up(part, span)
            span *= 2
        win = shift_down(part, half) + part
        win = win[POOL_HALO:POOL_HALO + tm]
        lo = jnp.clip(t - half, 0, seq_len)
        hi = jnp.clip(t - half + w, 0, seq_len)
        cnt = (hi - lo).astype(F32)
        pooled = win / cnt - e[POOL_HALO:POOL_HALO + tm]
        y = jnp.dot(pooled.astype(BF16), wp_ref[gi].astype(BF16), preferred_element_type=F32)
        o_ref[:, gi * grp:(gi + 1) * grp] = (y * ps_ref[:, gi * grp:(gi + 1) * grp]).astype(o_ref.dtype)

    aw = ya_ref.shape[1]
    o_ref[:, pool_w:pool_w + aw] = ya_ref[...]

    u = cc_ref[...] * cx_ref[...]
    u_before = jnp.where(i == 0, 0.0, ccp_ref[CONV_HALO - 1:CONV_HALO, :] * cxp_ref[CONV_HALO - 1:CONV_HALO, :])
    u_after = jnp.where(i == last, 0.0, ccn_ref[0:1, :] * cxn_ref[0:1, :])
    r = lax.broadcasted_iota(jnp.int32, (tm, 1), 0)
    um1 = jnp.where(r == 0, u_before, pltpu.roll(u, 1, axis=0))
    up1 = jnp.where(r == tm - 1, u_after, pltpu.roll(u, tm - 1, axis=0))
    conv = cw_ref[0:1, :] * um1 + cw_ref[1:2, :] * u + cw_ref[2:3, :] * up1
    o_ref[:, pool_w + aw:] = (cb_ref[...] * conv).astype(o_ref.dtype)


def _mixer(p, ya, w_pool, pool_scale, conv_w, cb_col0, tm=256):
    m = p.shape[0]
    tm = min(tm, m)
    blk = pool_scale.shape[-1]
    c0 = cb_col0 // blk
    nblk_pool, nblk_conv = m // POOL_HALO, m // CONV_HALO
    rp, rc = tm // POOL_HALO, tm // CONV_HALO

    cur = lambda c: pl.BlockSpec((tm, blk), lambda i: (i, c))
    prev_p = pl.BlockSpec((POOL_HALO, blk), lambda i: (jnp.maximum(i * rp - 1, 0), 0))
    next_p = pl.BlockSpec((POOL_HALO, blk), lambda i: (jnp.minimum((i + 1) * rp, nblk_pool - 1), 0))
    prev_c = lambda c: pl.BlockSpec((CONV_HALO, blk), lambda i: (jnp.maximum(i * rc - 1, 0), c))
    next_c = lambda c: pl.BlockSpec((CONV_HALO, blk), lambda i: (jnp.minimum((i + 1) * rc, nblk_conv - 1), c))
    aw = ya.shape[1]
    full = lambda a: pl.BlockSpec(a.shape, lambda i: (0,) * a.ndim)
    ps = pool_scale.reshape(1, blk)
    return pl.pallas_call(
        functools.partial(_mixer_kernel, seq_len=m),
        grid=(m // tm,),
        in_specs=[cur(0), prev_p, next_p,
                  cur(c0), cur(c0 + 1), prev_c(c0 + 1), next_c(c0 + 1),
                  cur(c0 + 2), prev_c(c0 + 2), next_c(c0 + 2),
                  pl.BlockSpec((tm, aw), lambda i: (i, 0)),
                  full(w_pool), full(ps), full(conv_w)],
        out_specs=pl.BlockSpec((tm, 2 * blk + aw), lambda i: (i, 0)),
        out_shape=jax.ShapeDtypeStruct((m, 2 * blk + aw), BF16),
        compiler_params=_params("parallel"),
        name="mixer",
    )(p, p, p, p, p, p, p, p, p, p, ya, w_pool, ps, conv_w)


def _down(act, wd, layer, tm=1024, tn=256):
    return _proj(act, wd, layer, wd.shape[-1], tm=tm, tn=tn, a_buffers=1)


def _moe_gate_up_kernel(te_ref, nv_ref, a_ref, wg_ref, wu_ref, o_ref):
    @pl.when(pl.program_id(1) < nv_ref[0])
    def _():
        _gate_up_kernel(a_ref, wg_ref, wu_ref, o_ref)


def _moe_down_kernel(te_ref, nv_ref, a_ref, w_ref, o_ref):
    @pl.when(pl.program_id(1) < nv_ref[0])
    def _():
        _proj_kernel(a_ref, w_ref, o_ref)


def _moe_call(kernel, a, weights, tile_expert, n_valid, n_out, out_dtype, tm, tn, name):
    r, k = a.shape
    n_tiles = r // tm
    row = lambda j, t, te, nv: (jnp.minimum(t, nv[0] - 1), 0)
    wspec = pl.BlockSpec((None, k, tn), lambda j, t, te, nv: (te[t], 0, j))
    return pl.pallas_call(
        kernel,
        grid_spec=pltpu.PrefetchScalarGridSpec(
            num_scalar_prefetch=2,
            grid=(n_out // tn, n_tiles),
            in_specs=[pl.BlockSpec((tm, k), row)] + [wspec] * len(weights),
            out_specs=pl.BlockSpec((tm, tn), lambda j, t, te, nv: (jnp.minimum(t, nv[0] - 1), j)),
        ),
        out_shape=jax.ShapeDtypeStruct((r, n_out), out_dtype),
        compiler_params=_params("parallel", "arbitrary"),
        name=name,
    )(tile_expert, n_valid, a, *weights)


def _moe_ffn(h, ridx, wg, wu, wd, tm=512, tn=512):
    m, d = h.shape
    n_slots = m * TOP_K
    r_max = n_slots + N_EXPERTS * tm
    n_tiles = r_max // tm
    e_flat = ridx[:, :TOP_K].reshape(n_slots)
    order = jnp.argsort(e_flat, stable=True).astype(jnp.int32)
    counts = jnp.sum((e_flat[:, None] == jnp.arange(N_EXPERTS)[None, :]).astype(jnp.int32), axis=0)
    padded = ((counts + tm - 1) // tm) * tm
    gend = jnp.cumsum(padded)
    gstart = gend - padded
    cstart = jnp.cumsum(counts) - counts
    n_valid = (gend[-1] // tm).astype(jnp.int32).reshape(1)
    tile_expert = jnp.minimum(jnp.searchsorted(gend, jnp.arange(n_tiles) * tm, side="right"),
                              N_EXPERTS - 1).astype(jnp.int32)
    last_e = tile_expert[jnp.maximum(n_valid[0] - 1, 0)]
    tile_expert = jnp.where(jnp.arange(n_tiles) < n_valid[0], tile_expert, last_e)
    rows = jnp.arange(r_max)
    row_e = tile_expert[rows // tm]
    off = rows - gstart[row_e]
    row_valid = (off < counts[row_e]) & (rows < gend[-1])
    src = jnp.clip(cstart[row_e] + off, 0, n_slots - 1)
    row_token = jnp.where(row_valid, order[src] // TOP_K, 0)
    rank = jnp.argsort(order).astype(jnp.int32)
    slot_row = (gstart[e_flat] + rank - cstart[e_flat]).reshape(m, TOP_K)

    hs = h.at[row_token].get(mode="promise_in_bounds")
    act = _moe_call(_moe_gate_up_kernel, hs, (wg, wu), tile_expert, n_valid, wg.shape[-1], BF16, tm, tn,
                    "moe_gate_up")
    ys = _moe_call(_moe_down_kernel, act, (wd,), tile_expert, n_valid, d, F32, tm, tn, "moe_down")
    return ys, slot_row


def _combine_kernel(x_ref, y1_ref, y2_ref, gw_ref, gate_ref, g_ref, o_ref):
    gw = gw_ref[...]
    y = gw[:, 0:1] * y1_ref[...] + gw[:, 1:2] * y2_ref[...]
    o_ref[...] = _rms(x_ref[...] + gate_ref[...] * y, g_ref[...])


def _combine_final(x, y1, y2, gw, gate, g, tm=256):
    m, d = x.shape
    row = pl.BlockSpec((tm, d), lambda i: (i, 0))
    vec = pl.BlockSpec((1, d), lambda i: (0, 0))
    return pl.pallas_call(
        _combine_kernel,
        grid=(m // tm,),
        in_specs=[row, row, row, pl.BlockSpec((tm, LANES), lambda i: (i, 0)), vec, vec],
        out_specs=row,
        out_shape=jax.ShapeDtypeStruct((m, d), F32),
        compiler_params=_params("parallel"),
        name="combine_final",
    )(x, y1, y2, gw, gate, g)


def _rope_tables(seq_len):
    rows = seq_len // GRID_W
    row = jnp.repeat(jnp.arange(rows), GRID_W).astype(F32)
    col = jnp.tile(jnp.arange(GRID_W), rows).astype(F32)
    n_axis = HEAD_DIM // 4
    inv = ROPE_BASE ** (-jnp.arange(n_axis, dtype=F32) / n_axis)
    ang = jnp.concatenate([row[:, None] * inv, col[:, None] * inv], axis=-1)
    cos, sin = jnp.cos(ang), jnp.sin(ang)
    return jnp.concatenate([cos, cos], axis=-1), jnp.concatenate([-sin, sin], axis=-1)


def kernel(x, c, ctx, c_ctx, w_mod, b_mod, g_mix, w_in, w_pool, pool_scale, g_q, g_k, conv_w, w_out, g_ffn,
           w_gate_dense, w_up_dense, w_down_dense, w_router, b_router, w_gate_exp, w_up_exp, w_down_exp, g_final):
    batch, seq, d = x.shape
    depth = w_mod.shape[0]
    assert batch == 1 and depth == 2
    n_ctx = ctx.shape[1]
    pool_w = pool_scale.shape[-1]
    conv_wd = conv_w.shape[-1]
    q_w = (d // (2 * HEAD_DIM)) * HEAD_DIM
    kv_w = q_w // KV_GROUP
    q_off = pool_w
    k_off = q_off + q_w
    cb_off = k_off + 2 * kv_w
    in_w = cb_off + 3 * conv_wd

    xs, cs = x[0], ctx[0]
    cvec = jnp.zeros((SUBLANES, d), F32).at[0].set(c[0]).at[1].set(c_ctx)
    mods = _mods(cvec, w_mod, b_mod)

    def mod(l, row, k):
        return mods[l, row:row + 1, k * d:(k + 1) * d]

    cosf, sinf = _rope_tables(seq)
    cos_c, sin_c = jnp.ones((n_ctx, HEAD_DIM), F32), jnp.zeros((n_ctx, HEAD_DIM), F32)
    qscale = HEAD_DIM ** -0.5 * LOG2E

    pend_x = pend_c = None
    for l in range(depth):
        last = l == depth - 1
        vec = lambda a: a[l].reshape(1, -1)
        gq, gk = vec(g_q) * qscale, vec(g_k)

        def first_norm(stream, pend, row):
            if pend is None:
                (h,) = _norm(stream, vec(g_mix), mod(l, row, 0), mod(l, row, 1))
                return stream, h
            return _norm(stream, vec(g_mix), mod(l, row, 0), mod(l, row, 1), resid=pend)

        cs, hc = first_norm(cs, pend_c, 1)
        if last:
            pkv = _proj(hc, w_in, l, 2 * kv_w, col0=k_off)
            kc, vc = _qkv_prep(pkv, cos_c, sin_c, gq, gk, None, 0)
        else:
            pc = _proj(hc, w_in, l, in_w)
            qc, kc, vc = _qkv_prep(pc, cos_c, sin_c, gq, gk, q_off, k_off)
            yac = _flash_t(_heads_t(qc), kc, vc.T)
            mixc = _mixer(pc, yac, w_pool[l], pool_scale[l], conv_w[l], cb_off)
            ymc = _proj(mixc, w_out, l, d)

        xs, hx = first_norm(xs, pend_x, 0)
        px = _proj(hx, w_in, l, in_w)
        qx, kx, vx = _qkv_prep(px, cosf, sinf, gq, gk, q_off, k_off)
        k_all = jnp.concatenate([kc, kx], axis=0)
        v_all = jnp.concatenate([vc, vx], axis=0)
        yax = _flash_t(_heads_t(qx), k_all, v_all.T)
        mixx = _mixer(px, yax, w_pool[l], pool_scale[l], conv_w[l], cb_off)
        ymx = _proj(mixx, w_out, l, d)

        i = l // 2
        if l % 2 == 0:
            def ffn(stream, ym, row):
                xn, h = _norm(stream, vec(g_ffn), mod(l, row, 3), mod(l, row, 4), resid=(ym, mod(l, row, 2)))
                act = _gate_up(h, w_gate_dense, w_up_dense, i)
                return xn, _down(act, w_down_dense, i)

            xs, yfx = ffn(xs, ymx, 0)
            pend_x = (yfx, mod(l, 0, 5))
            if not last:
                cs, yfc = ffn(cs, ymc, 1)
                pend_c = (yfc, mod(l, 1, 5))
        else:
            wr = jnp.zeros((d, LANES), F32).at[:, :N_EXPERTS].set(w_router[i])
            br = jnp.full((1, LANES), NEG_BIG, F32).at[0, :N_EXPERTS].set(b_router[i])
            xs, hf, ridx, rgw = _norm(xs, vec(g_ffn), mod(l, 0, 3), mod(l, 0, 4), resid=(ymx, mod(l, 0, 2)),
                                      router=(wr, br), mode="route")
            ys, slot_row = _moe_ffn(hf, ridx, w_gate_exp[i], w_up_exp[i], w_down_exp[i])
            y1 = ys.at[slot_row[:, 0]].get(mode="promise_in_bounds")
            y2 = ys.at[slot_row[:, 1]].get(mode="promise_in_bounds")
            pend_x = (y1, y2, rgw, mod(l, 0, 5))

    if len(pend_x) == 4:
        out = _combine_final(xs, pend_x[0], pend_x[1], pend_x[2], pend_x[3], g_final.reshape(1, -1))
    else:
        (out,) = _norm(xs, g_final.reshape(1, -1), resid=pend_x, mode="final")
    return out[None]
```

```python
import functools

import jax
import jax.numpy as jnp
from jax import lax
from jax.experimental import pallas as pl
from jax.experimental.pallas import tpu as pltpu

F32 = jnp.float32
BF16 = jnp.bfloat16

POOL_WINDOWS = (2, 4, 8, 16)
HEAD_DIM = 128
KV_GROUP = 4
N_MOD = 6
N_EXPERTS = 8
TOP_K = 2
GRID_W = 64
ROPE_BASE = 10000.0
EPS = 1e-6

LANES = 128
SUBLANES = 8
VMEM_LIMIT_BYTES = 56 * 1024 * 1024

POOL_HALO = 16
CONV_HALO = 8
NEG_BIG = -1e30
LOG2E = 1.4426950408889634


def _params(*sem):
    return pltpu.CompilerParams(dimension_semantics=sem, vmem_limit_bytes=VMEM_LIMIT_BYTES)


def _silu(x):
    return x * jax.nn.sigmoid(x)


def _mods_kernel(c_ref, w_ref, b_ref, o_ref):
    s = _silu(c_ref[...]).astype(BF16)
    w = w_ref[...].astype(BF16)
    o_ref[...] = jnp.dot(s, w, preferred_element_type=F32) + b_ref[...]


def _mods(cvec, w_mod, b_mod, tn=512):
    depth, d, n = w_mod.shape
    return pl.pallas_call(
        _mods_kernel,
        grid=(depth, n // tn),
        in_specs=[
            pl.BlockSpec((SUBLANES, d), lambda l, j: (0, 0)),
            pl.BlockSpec((None, d, tn), lambda l, j: (l, 0, j)),
            pl.BlockSpec((None, 1, tn), lambda l, j: (l, 0, j)),
        ],
        out_specs=pl.BlockSpec((None, SUBLANES, tn), lambda l, j: (l, 0, j)),
        out_shape=jax.ShapeDtypeStruct((depth, SUBLANES, n), F32),
        compiler_params=_params("parallel", "parallel"),
        name="mods",
    )(cvec, w_mod, b_mod.reshape(depth, 1, n))


def _rms(x, g):
    return (x * lax.rsqrt(jnp.mean(x * x, axis=-1, keepdims=True) + EPS)) * g


def _norm_kernel(*refs, has_resid, mode):
    it = iter(refs)
    x_ref = next(it)
    x = x_ref[...]
    if has_resid:
        y_ref, gate_ref = next(it), next(it)
        x = x + gate_ref[...] * y_ref[...].astype(F32)
    g_ref = next(it)
    if mode == "final":
        o_ref = next(it)
        o_ref[...] = _rms(x, g_ref[...])
        return
    shift_ref, scale_ref = next(it), next(it)
    if mode == "route":
        wr_ref, br_ref = next(it), next(it)
    if has_resid:
        xo_ref = next(it)
        xo_ref[...] = x
    h_ref = next(it)
    h = _rms(x, g_ref[...]) * (1.0 + scale_ref[...]) + shift_ref[...]
    h_ref[...] = h.astype(BF16)
    if mode == "route":
        idx_ref, gw_ref = next(it), next(it)
        logits = jnp.dot(h, wr_ref[...], preferred_element_type=F32,
                         precision=lax.Precision.HIGHEST) + br_ref[...]
        lane = lax.broadcasted_iota(jnp.int32, logits.shape, 1).astype(F32)
        m1 = jnp.max(logits, axis=-1, keepdims=True)
        i1 = jnp.min(jnp.where(logits == m1, lane, float(LANES)), axis=-1, keepdims=True)
        rest = jnp.where(lane == i1, -jnp.inf, logits)
        m2 = jnp.max(rest, axis=-1, keepdims=True)
        i2 = jnp.min(jnp.where(rest == m2, lane, float(LANES)), axis=-1, keepdims=True)
        e = jnp.exp(m2 - m1)
        den = 1.0 + e
        idx_ref[...] = jnp.where(lane == 0.0, i1, jnp.where(lane == 1.0, i2, 0.0)).astype(jnp.int32)
        gw_ref[...] = jnp.where(lane == 0.0, 1.0 / den, jnp.where(lane == 1.0, e / den, 0.0))


def _norm(x, g, shift=None, scale=None, resid=None, router=None, mode="mod", tm=256):
    m, d = x.shape
    tm = min(tm, m)
    row = pl.BlockSpec((tm, d), lambda i: (i, 0))
    vec = pl.BlockSpec((1, d), lambda i: (0, 0))
    lane_blk = pl.BlockSpec((tm, LANES), lambda i: (i, 0))
    args, in_specs = [x], [row]
    if resid is not None:
        args += [resid[0], resid[1]]
        in_specs += [row, vec]
    args.append(g)
    in_specs.append(vec)
    out_shape, out_specs = [], []
    if mode != "final":
        args += [shift, scale]
        in_specs += [vec, vec]
        if mode == "route":
            args += [router[0], router[1]]
            in_specs += [pl.BlockSpec((d, LANES), lambda i: (0, 0)), pl.BlockSpec((1, LANES), lambda i: (0, 0))]
        if resid is not None:
            out_shape.append(jax.ShapeDtypeStruct((m, d), F32))
            out_specs.append(row)
        out_shape.append(jax.ShapeDtypeStruct((m, d), BF16))
        out_specs.append(row)
        if mode == "route":
            out_shape += [jax.ShapeDtypeStruct((m, LANES), jnp.int32), jax.ShapeDtypeStruct((m, LANES), F32)]
            out_specs += [lane_blk, lane_blk]
    else:
        out_shape.append(jax.ShapeDtypeStruct((m, d), F32))
        out_specs.append(row)
    return pl.pallas_call(
        functools.partial(_norm_kernel, has_resid=resid is not None, mode=mode),
        grid=(m // tm,),
        in_specs=in_specs,
        out_specs=out_specs,
        out_shape=out_shape,
        compiler_params=_params("parallel"),
        name="norm_" + mode,
    )(*args)


def _proj_kernel(a_ref, w_ref, o_ref):
    o_ref[...] = jnp.dot(a_ref[...], w_ref[...].astype(BF16), preferred_element_type=F32).astype(o_ref.dtype)


def _proj(a, w, layer, n_out, col0=0, tm=1024, tn=512, out_dtype=F32, a_buffers=2):
    m, k = a.shape
    tm = min(tm, m)
    cb = col0 // tn
    a_mode = {} if a_buffers == 2 else {"pipeline_mode": pl.Buffered(a_buffers)}
    return pl.pallas_call(
        _proj_kernel,
        grid=(m // tm, n_out // tn),
        in_specs=[
            pl.BlockSpec((tm, k), lambda i, j: (i, 0), **a_mode),
            pl.BlockSpec((None, k, tn), lambda i, j: (layer, 0, j + cb)),
        ],
        out_specs=pl.BlockSpec((tm, tn), lambda i, j: (i, j)),
        out_shape=jax.ShapeDtypeStruct((m, n_out), out_dtype),
        compiler_params=_params("parallel", "arbitrary"),
        name="proj",
    )(a, w)


def _gate_up_kernel(a_ref, wg_ref, wu_ref, o_ref):
    a = a_ref[...]
    g = jnp.dot(a, wg_ref[...].astype(BF16), preferred_element_type=F32)
    u = jnp.dot(a, wu_ref[...].astype(BF16), preferred_element_type=F32)
    o_ref[...] = (_silu(g) * u).astype(o_ref.dtype)


def _gate_up(a, wg, wu, layer, tm=1024, tn=256):
    m, k = a.shape
    n = wg.shape[-1]
    tm = min(tm, m)
    wspec = pl.BlockSpec((None, k, tn), lambda i, j: (layer, 0, j))
    return pl.pallas_call(
        _gate_up_kernel,
        grid=(m // tm, n // tn),
        in_specs=[pl.BlockSpec((tm, k), lambda i, j: (i, 0)), wspec, wspec],
        out_specs=pl.BlockSpec((tm, tn), lambda i, j: (i, j)),
        out_shape=jax.ShapeDtypeStruct((m, n), BF16),
        compiler_params=_params("parallel", "arbitrary"),
        name="gate_up",
    )(a, wg, wu)


def _qkv_kernel(*refs, want_q):
    if want_q:
        q0_ref, q1_ref, kv_ref, cos_ref, sin_ref, gq_ref, gk_ref, qt_ref, k_ref, vt_ref = refs
    else:
        kv_ref, cos_ref, sin_ref, gk_ref, k_ref, vt_ref = refs
    cos, sin = cos_ref[...], sin_ref[...]
    tm = cos.shape[0]

    def head(x, g):
        y = _rms(x, g)
        return y * cos + pltpu.roll(y, HEAD_DIM // 2, axis=1) * sin

    if want_q:
        heads_per_blk = q0_ref.shape[1] // HEAD_DIM
        for b, src in enumerate((q0_ref, q1_ref)):
            for hh in range(heads_per_blk):
                lo = hh * HEAD_DIM
                kv, g = divmod(b * heads_per_blk + hh, KV_GROUP)
                qh = head(src[:, lo:lo + HEAD_DIM], gq_ref[...])
                qt_ref[kv, :, g * tm:(g + 1) * tm] = qh.T.astype(BF16)
    kvw = k_ref.shape[1]
    for hh in range(kvw // HEAD_DIM):
        lo = hh * HEAD_DIM
        k_ref[:, lo:lo + HEAD_DIM] = head(kv_ref[:, lo:lo + HEAD_DIM], gk_ref[...]).astype(BF16)
    vt_ref[...] = kv_ref[:, kvw:].T.astype(BF16)


def _qkv_prep(p, cosf, sinf, gq_scaled, gk, q_col0, kv_col0, tm=512):
    m = p.shape[0]
    tm = min(tm, m)
    blk = 2 * KV_GROUP * HEAD_DIM
    kvw = blk // 2
    hkv = kvw // HEAD_DIM
    want_q = q_col0 is not None
    pspec = lambda c: pl.BlockSpec((tm, blk), lambda i: (i, c))
    tab = pl.BlockSpec((tm, HEAD_DIM), lambda i: (i, 0))
    gvec = pl.BlockSpec((1, HEAD_DIM), lambda i: (0, 0))
    kv_specs = [pl.BlockSpec((tm, kvw), lambda i: (i, 0)), pl.BlockSpec((kvw, tm), lambda i: (0, i))]
    kv_shapes = [jax.ShapeDtypeStruct((m, kvw), BF16), jax.ShapeDtypeStruct((kvw, m), BF16)]
    if want_q:
        args = (p, p, p, cosf, sinf, gq_scaled, gk)
        in_specs = [pspec(q_col0 // blk), pspec(q_col0 // blk + 1), pspec(kv_col0 // blk), tab, tab, gvec, gvec]
        out_specs = [pl.BlockSpec((hkv, None, HEAD_DIM, KV_GROUP * tm), lambda i: (0, i, 0, 0))] + kv_specs
        out_shape = [jax.ShapeDtypeStruct((hkv, m // tm, HEAD_DIM, KV_GROUP * tm), BF16)] + kv_shapes
    else:
        args = (p, cosf, sinf, gk)
        in_specs = [pspec(kv_col0 // blk), tab, tab, gvec]
        out_specs = kv_specs
        out_shape = kv_shapes
    return pl.pallas_call(
        functools.partial(_qkv_kernel, want_q=want_q),
        grid=(m // tm,),
        in_specs=in_specs,
        out_specs=out_specs,
        out_shape=out_shape,
        compiler_params=_params("parallel"),
        name="qkv_prep",
    )(*args)


def _flash_t_kernel(qn_ref, kn_ref, qc_ref, kc_ref, vt_ref, o_ref,
                    s_a, s_b, mx_a, mx_b, m_ref, l_ref, acc_ref):
    h, i, j = pl.program_id(0), pl.program_id(1), pl.program_id(2)
    nq, nk = pl.num_programs(1), pl.num_programs(2)
    step = (h * nq + i) * nk + j

    def scores(k_ref, q_ref, s_ref, mx_ref):
        s = jnp.dot(k_ref[...], q_ref[...], preferred_element_type=F32)
        s_ref[...] = s
        mx_ref[...] = jnp.max(s, axis=0, keepdims=True)

    @pl.when(step == 0)
    def _():
        scores(kc_ref, qc_ref, s_a, mx_a)

    @pl.when(j == 0)
    def _():
        m_ref[...] = jnp.full(m_ref.shape, -jnp.inf, F32)
        l_ref[...] = jnp.zeros(l_ref.shape, F32)
        acc_ref[...] = jnp.zeros(acc_ref.shape, F32)

    def body(s_cur, mx_cur, s_nxt, mx_nxt):
        scores(kn_ref, qn_ref, s_nxt, mx_nxt)
        m_prev = m_ref[...]
        m_new = jnp.maximum(m_prev, mx_cur[...])
        alpha = jnp.exp2(m_prev - m_new)
        p = jnp.exp2(s_cur[...] - m_new)
        l_ref[...] = alpha * l_ref[...] + jnp.sum(p, axis=0, keepdims=True)
        acc_ref[...] = alpha * acc_ref[...] + jnp.dot(vt_ref[...], p.astype(BF16), preferred_element_type=F32)
        m_ref[...] = m_new

    @pl.when(step % 2 == 0)
    def _():
        body(s_a, mx_a, s_b, mx_b)

    @pl.when(step % 2 == 1)
    def _():
        body(s_b, mx_b, s_a, mx_a)

    @pl.when(j == nk - 1)
    def _():
        tq = o_ref.shape[0]
        o = acc_ref[...] / l_ref[...]
        for g in range(KV_GROUP):
            o_ref[:, g * HEAD_DIM:(g + 1) * HEAD_DIM] = o[:, g * tq:(g + 1) * tq].T.astype(o_ref.dtype)


def _flash_t(qt, k, vt, tk=1408):
    hkv, nq, _, nqc = qt.shape
    tq = nqc // KV_GROUP
    lk = k.shape[0]
    tk = min(tk, lk)
    nk = lk // tk
    gw = KV_GROUP * HEAD_DIM

    def nxt(h, i, j):
        j1 = j + 1
        i1 = i + j1 // nk
        h1 = h + i1 // nq
        return jnp.minimum(h1, hkv - 1), i1 % nq, j1 % nk

    def qn_map(h, i, j):
        h1, i1, _ = nxt(h, i, j)
        return h1, i1, 0, 0

    def kn_map(h, i, j):
        h1, _, j1 = nxt(h, i, j)
        return j1, h1

    qspec = lambda m: pl.BlockSpec((None, None, HEAD_DIM, nqc), m)
    return pl.pallas_call(
        _flash_t_kernel,
        grid=(hkv, nq, nk),
        in_specs=[
            qspec(qn_map),
            pl.BlockSpec((tk, HEAD_DIM), kn_map),
            qspec(lambda h, i, j: (h, i, 0, 0)),
            pl.BlockSpec((tk, HEAD_DIM), lambda h, i, j: (0, h)),
            pl.BlockSpec((HEAD_DIM, tk), lambda h, i, j: (h, j)),
        ],
        out_specs=pl.BlockSpec((tq, gw), lambda h, i, j: (i, h)),
        out_shape=jax.ShapeDtypeStruct((nq * tq, hkv * gw), BF16),
        scratch_shapes=[pltpu.VMEM((tk, nqc), F32), pltpu.VMEM((tk, nqc), F32),
                        pltpu.VMEM((1, nqc), F32), pltpu.VMEM((1, nqc), F32),
                        pltpu.VMEM((1, nqc), F32), pltpu.VMEM((1, nqc), F32),
                        pltpu.VMEM((HEAD_DIM, nqc), F32)],
        compiler_params=_params("arbitrary", "arbitrary", "arbitrary"),
        name="flash_t",
    )(qt, k, qt, k, vt)


def _mixer_kernel(pc_ref, pp_ref, pn_ref, cb_ref, cc_ref, ccp_ref, ccn_ref, cx_ref, cxp_ref, cxn_ref,
                  ya_ref, wp_ref, ps_ref, cw_ref, o_ref, *, seq_len):
    i = pl.program_id(0)
    last = pl.num_programs(0) - 1
    tm = pc_ref.shape[0]
    n_ext = tm + 2 * POOL_HALO
    pool_w = pc_ref.shape[1]
    grp = pool_w // len(POOL_WINDOWS)

    prev = jnp.where(i == 0, 0.0, pp_ref[...])
    nxt = jnp.where(i == last, 0.0, pn_ref[...])
    ext = jnp.concatenate([prev, pc_ref[...], nxt], axis=0)
    t = (i * tm + lax.broadcasted_iota(jnp.int32, (tm, 1), 0))

    def shift_up(x, k):
        return pltpu.roll(x, n_ext - k, axis=0)

    def shift_down(x, k):
        return pltpu.roll(x, k, axis=0)

    for gi, w in enumerate(POOL_WINDOWS):
        e = ext[:, gi * grp:(gi + 1) * grp]
        half = w // 2
        part, span = e, 1
        while span < half:
            part = part + shift_up(part, span)
            span *= 2
        win = shift_down(part, half) + part
        win = win[POOL_HALO:POOL_HALO + tm]
        lo = jnp.clip(t - half, 0, seq_len)
        hi = jnp.clip(t - half + w, 0, seq_len)
        cnt = (hi - lo).astype(F32)
        pooled = win / cnt - e[POOL_HALO:POOL_HALO + tm]
        y = jnp.dot(pooled.astype(BF16), wp_ref[gi].astype(BF16), preferred_element_type=F32)
        o_ref[:, gi * grp:(gi + 1) * grp] = (y * ps_ref[:, gi * grp:(gi + 1) * grp]).astype(o_ref.dtype)

    aw = ya_ref.shape[1]
    o_ref[:, pool_w:pool_w + aw] = ya_ref[...]

    u = cc_ref[...] * cx_ref[...]
    u_before = jnp.where(i == 0, 0.0, ccp_ref[CONV_HALO - 1:CONV_HALO, :] * cxp_ref[CONV_HALO - 1:CONV_HALO, :])
    u_after = jnp.where(i == last, 0.0, ccn_ref[0:1, :] * cxn_ref[0:1, :])
    r = lax.broadcasted_iota(jnp.int32, (tm, 1), 0)
    um1 = jnp.where(r == 0, u_before, pltpu.roll(u, 1, axis=0))
    up1 = jnp.where(r == tm - 1, u_after, pltpu.roll(u, tm - 1, axis=0))
    conv = cw_ref[0:1, :] * um1 + cw_ref[1:2, :] * u + cw_ref[2:3, :] * up1
    o_ref[:, pool_w + aw:] = (cb_ref[...] * conv).astype(o_ref.dtype)


def _mixer(p, ya, w_pool, pool_scale, conv_w, cb_col0, tm=256):
    m = p.shape[0]
    tm = min(tm, m)
    blk = pool_scale.shape[-1]
    c0 = cb_col0 // blk
    nblk_pool, nblk_conv = m // POOL_HALO, m // CONV_HALO
    rp, rc = tm // POOL_HALO, tm // CONV_HALO

    cur = lambda c: pl.BlockSpec((tm, blk), lambda i: (i, c))
    prev_p = pl.BlockSpec((POOL_HALO, blk), lambda i: (jnp.maximum(i * rp - 1, 0), 0))
    next_p = pl.BlockSpec((POOL_HALO, blk), lambda i: (jnp.minimum((i + 1) * rp, nblk_pool - 1), 0))
    prev_c = lambda c: pl.BlockSpec((CONV_HALO, blk), lambda i: (jnp.maximum(i * rc - 1, 0), c))
    next_c = lambda c: pl.BlockSpec((CONV_HALO, blk), lambda i: (jnp.minimum((i + 1) * rc, nblk_conv - 1), c))
    aw = ya.shape[1]
    full = lambda a: pl.BlockSpec(a.shape, lambda i: (0,) * a.ndim)
    ps = pool_scale.reshape(1, blk)
    return pl.pallas_call(
        functools.partial(_mixer_kernel, seq_len=m),
        grid=(m // tm,),
        in_specs=[cur(0), prev_p, next_p,
                  cur(c0), cur(c0 + 1), prev_c(c0 + 1), next_c(c0 + 1),
                  cur(c0 + 2), prev_c(c0 + 2), next_c(c0 + 2),
                  pl.BlockSpec((tm, aw), lambda i: (i, 0)),
                  full(w_pool), full(ps), full(conv_w)],
        out_specs=pl.BlockSpec((tm, 2 * blk + aw), lambda i: (i, 0)),
        out_shape=jax.ShapeDtypeStruct((m, 2 * blk + aw), BF16),
        compiler_params=_params("parallel"),
        name="mixer",
    )(p, p, p, p, p, p, p, p, p, p, ya, w_pool, ps, conv_w)


def _down(act, wd, layer, tm=1024, tn=256):
    return _proj(act, wd, layer, wd.shape[-1], tm=tm, tn=tn, a_buffers=1)


def _moe_tile_kernel(te_ref, nv_ref, a_ref, *refs, body):
    o_ref = refs[-1]
    valid = pl.program_id(1) < nv_ref[0]

    @pl.when(valid)
    def _():
        body(a_ref, *refs)

    @pl.when(jnp.logical_not(valid))
    def _():
        o_ref[...] = jnp.zeros(o_ref.shape, o_ref.dtype)


_moe_gate_up_kernel = functools.partial(_moe_tile_kernel, body=_gate_up_kernel)
_moe_down_kernel = functools.partial(_moe_tile_kernel, body=_proj_kernel)


def _moe_call(kernel, a, weights, tile_expert, n_valid, n_out, out_dtype, tm, tn, name):
    r, k = a.shape
    n_tiles = r // tm
    row = lambda j, t, te, nv: (jnp.minimum(t, nv[0] - 1), 0)
    wspec = pl.BlockSpec((None, k, tn), lambda j, t, te, nv: (te[t], 0, j))
    return pl.pallas_call(
        kernel,
        grid_spec=pltpu.PrefetchScalarGridSpec(
            num_scalar_prefetch=2,
            grid=(n_out // tn, n_tiles),
            in_specs=[pl.BlockSpec((tm, k), row)] + [wspec] * len(weights),
            out_specs=pl.BlockSpec((tm, tn), lambda j, t, te, nv: (t, j)),
        ),
        out_shape=jax.ShapeDtypeStruct((r, n_out), out_dtype),
        compiler_params=_params("parallel", "arbitrary"),
        name=name,
    )(tile_expert, n_valid, a, *weights)


def _moe_ffn(h, ridx, wg, wu, wd, tm=512, tn=512, tn_down=1024):
    m, d = h.shape
    n_slots = m * TOP_K
    r_max = n_slots + N_EXPERTS * tm
    n_tiles = r_max // tm
    e_flat = ridx[:, :TOP_K].reshape(n_slots)
    order = jnp.argsort(e_flat, stable=True).astype(jnp.int32)
    counts = jnp.sum((e_flat[:, None] == jnp.arange(N_EXPERTS)[None, :]).astype(jnp.int32), axis=0)
    padded = ((counts + tm - 1) // tm) * tm
    gend = jnp.cumsum(padded)
    gstart = gend - padded
    cstart = jnp.cumsum(counts) - counts
    n_valid = (gend[-1] // tm).astype(jnp.int32).reshape(1)
    tile_expert = jnp.minimum(jnp.searchsorted(gend, jnp.arange(n_tiles) * tm, side="right"),
                              N_EXPERTS - 1).astype(jnp.int32)
    last_e = tile_expert[jnp.maximum(n_valid[0] - 1, 0)]
    tile_expert = jnp.where(jnp.arange(n_tiles) < n_valid[0], tile_expert, last_e)
    rows = jnp.arange(r_max)
    row_e = tile_expert[rows // tm]
    off = rows - gstart[row_e]
    row_valid = (off < counts[row_e]) & (rows < gend[-1])
    src = jnp.clip(cstart[row_e] + off, 0, n_slots - 1)
    row_token = jnp.where(row_valid, order[src] // TOP_K, 0)
    rank = jnp.argsort(order).astype(jnp.int32)
    slot_row = (gstart[e_flat] + rank - cstart[e_flat]).reshape(m, TOP_K)

    hs = h.at[row_token].get(mode="promise_in_bounds")
    act = _moe_call(_moe_gate_up_kernel, hs, (wg, wu), tile_expert, n_valid, wg.shape[-1], BF16, tm, tn,
                    "moe_gate_up")
    ys = _moe_call(_moe_down_kernel, act, (wd,), tile_expert, n_valid, d, BF16, tm, tn_down, "moe_down")
    return ys, slot_row


def _combine_kernel(x_ref, y1_ref, y2_ref, gw_ref, gate_ref, g_ref, o_ref):
    gw = gw_ref[...]
    y = gw[:, 0:1] * y1_ref[...].astype(F32) + gw[:, 1:2] * y2_ref[...].astype(F32)
    o_ref[...] = _rms(x_ref[...] + gate_ref[...] * y, g_ref[...])


def _combine_final(x, y1, y2, gw, gate, g, tm=256):
    m, d = x.shape
    row = pl.BlockSpec((tm, d), lambda i: (i, 0))
    vec = pl.BlockSpec((1, d), lambda i: (0, 0))
    return pl.pallas_call(
        _combine_kernel,
        grid=(m // tm,),
        in_specs=[row, row, row, pl.BlockSpec((tm, LANES), lambda i: (i, 0)), vec, vec],
        out_specs=row,
        out_shape=jax.ShapeDtypeStruct((m, d), F32),
        compiler_params=_params("parallel"),
        name="combine_final",
    )(x, y1, y2, gw, gate, g)


def _rope_tables(seq_len):
    rows = seq_len // GRID_W
    row = jnp.repeat(jnp.arange(rows), GRID_W).astype(F32)
    col = jnp.tile(jnp.arange(GRID_W), rows).astype(F32)
    n_axis = HEAD_DIM // 4
    inv = ROPE_BASE ** (-jnp.arange(n_axis, dtype=F32) / n_axis)
    ang = jnp.concatenate([row[:, None] * inv, col[:, None] * inv], axis=-1)
    cos, sin = jnp.cos(ang), jnp.sin(ang)
    return jnp.concatenate([cos, cos], axis=-1), jnp.concatenate([-sin, sin], axis=-1)


def kernel(x, c, ctx, c_ctx, w_mod, b_mod, g_mix, w_in, w_pool, pool_scale, g_q, g_k, conv_w, w_out, g_ffn,
           w_gate_dense, w_up_dense, w_down_dense, w_router, b_router, w_gate_exp, w_up_exp, w_down_exp, g_final):
    batch, seq, d = x.shape
    depth = w_mod.shape[0]
    assert batch == 1 and depth == 2
    n_ctx = ctx.shape[1]
    pool_w = pool_scale.shape[-1]
    conv_wd = conv_w.shape[-1]
    q_w = (d // (2 * HEAD_DIM)) * HEAD_DIM
    kv_w = q_w // KV_GROUP
    q_off = pool_w
    k_off = q_off + q_w
    cb_off = k_off + 2 * kv_w
    in_w = cb_off + 3 * conv_wd

    xs, cs = x[0], ctx[0]
    cvec = jnp.zeros((SUBLANES, d), F32).at[0].set(c[0]).at[1].set(c_ctx)
    mods = _mods(cvec, w_mod, b_mod)

    def mod(l, row, k):
        return mods[l, row:row + 1, k * d:(k + 1) * d]

    cosf, sinf = _rope_tables(seq)
    cos_c, sin_c = jnp.ones((n_ctx, HEAD_DIM), F32), jnp.zeros((n_ctx, HEAD_DIM), F32)
    qscale = HEAD_DIM ** -0.5 * LOG2E

    pend_x = pend_c = None
    for l in range(depth):
        last = l == depth - 1
        vec = lambda a: a[l].reshape(1, -1)
        gq, gk = vec(g_q) * qscale, vec(g_k)

        def first_norm(stream, pend, row):
            if pend is None:
                (h,) = _norm(stream, vec(g_mix), mod(l, row, 0), mod(l, row, 1))
                return stream, h
            return _norm(stream, vec(g_mix), mod(l, row, 0), mod(l, row, 1), resid=pend)

        cs, hc = first_norm(cs, pend_c, 1)
        if last:
            pkv = _proj(hc, w_in, l, 2 * kv_w, col0=k_off)
            kc, vtc = _qkv_prep(pkv, cos_c, sin_c, gq, gk, None, 0)
        else:
            pc = _proj(hc, w_in, l, in_w)
            qtc, kc, vtc = _qkv_prep(pc, cos_c, sin_c, gq, gk, q_off, k_off)
            yac = _flash_t(qtc, kc, vtc)
            mixc = _mixer(pc, yac, w_pool[l], pool_scale[l], conv_w[l], cb_off)
            ymc = _proj(mixc, w_out, l, d)

        xs, hx = first_norm(xs, pend_x, 0)
        px = _proj(hx, w_in, l, in_w)
        qtx, kx, vtx = _qkv_prep(px, cosf, sinf, gq, gk, q_off, k_off)
        k_all = jnp.concatenate([kc, kx], axis=0)
        vt_all = jnp.concatenate([vtc, vtx], axis=1)
        yax = _flash_t(qtx, k_all, vt_all)
        mixx = _mixer(px, yax, w_pool[l], pool_scale[l], conv_w[l], cb_off)
        ymx = _proj(mixx, w_out, l, d)

        i = l // 2
        if l % 2 == 0:
            def ffn(stream, ym, row):
                xn, h = _norm(stream, vec(g_ffn), mod(l, row, 3), mod(l, row, 4), resid=(ym, mod(l, row, 2)))
                act = _gate_up(h, w_gate_dense, w_up_dense, i)
                return xn, _down(act, w_down_dense, i)

            xs, yfx = ffn(xs, ymx, 0)
            pend_x = (yfx, mod(l, 0, 5))
            if not last:
                cs, yfc = ffn(cs, ymc, 1)
                pend_c = (yfc, mod(l, 1, 5))
        else:
            wr = jnp.zeros((d, LANES), F32).at[:, :N_EXPERTS].set(w_router[i])
            br = jnp.full((1, LANES), NEG_BIG, F32).at[0, :N_EXPERTS].set(b_router[i])
            xs, hf, ridx, rgw = _norm(xs, vec(g_ffn), mod(l, 0, 3), mod(l, 0, 4), resid=(ymx, mod(l, 0, 2)),
                                      router=(wr, br), mode="route")
            ys, slot_row = _moe_ffn(hf, ridx, w_gate_exp[i], w_up_exp[i], w_down_exp[i])
            y1 = ys.at[slot_row[:, 0]].get(mode="promise_in_bounds")
            y2 = ys.at[slot_row[:, 1]].get(mode="promise_in_bounds")
            pend_x = (y1, y2, rgw, mod(l, 0, 5))

    if len(pend_x) == 4:
        out = _combine_final(xs, pend_x[0], pend_x[1], pend_x[2], pend_x[3], g_final.reshape(1, -1))
    else:
        (out,) = _norm(xs, g_final.reshape(1, -1), resid=pend_x, mode="final")
    return out[None]
```

```python
import functools

import jax
import jax.numpy as jnp
from jax import lax
from jax.experimental import pallas as pl
from jax.experimental.pallas import tpu as pltpu

F32 = jnp.float32
BF16 = jnp.bfloat16

POOL_WINDOWS = (2, 4, 8, 16)
HEAD_DIM = 128
KV_GROUP = 4
N_MOD = 6
N_EXPERTS = 8
TOP_K = 2
GRID_W = 64
ROPE_BASE = 10000.0
EPS = 1e-6

LANES = 128
SUBLANES = 8
VMEM_LIMIT_BYTES = 56 * 1024 * 1024

POOL_HALO = 16
CONV_HALO = 8
NEG_BIG = -1e30
LOG2E = 1.4426950408889634


def _params(*sem):
    return pltpu.CompilerParams(dimension_semantics=sem, vmem_limit_bytes=VMEM_LIMIT_BYTES)


def _silu(x):
    return x * jax.nn.sigmoid(x)


def _mods_kernel(c_ref, w_ref, b_ref, o_ref):
    s = _silu(c_ref[...]).astype(BF16)
    w = w_ref[...].astype(BF16)
    o_ref[...] = jnp.dot(s, w, preferred_element_type=F32) + b_ref[...]


def _mods(cvec, w_mod, b_mod, tn=512):
    depth, d, n = w_mod.shape
    return pl.pallas_call(
        _mods_kernel,
        grid=(depth, n // tn),
        in_specs=[
            pl.BlockSpec((SUBLANES, d), lambda l, j: (0, 0)),
            pl.BlockSpec((None, d, tn), lambda l, j: (l, 0, j)),
            pl.BlockSpec((None, 1, tn), lambda l, j: (l, 0, j)),
        ],
        out_specs=pl.BlockSpec((None, SUBLANES, tn), lambda l, j: (l, 0, j)),
        out_shape=jax.ShapeDtypeStruct((depth, SUBLANES, n), F32),
        compiler_params=_params("parallel", "parallel"),
        name="mods",
    )(cvec, w_mod, b_mod.reshape(depth, 1, n))


def _rms(x, g):
    return (x * lax.rsqrt(jnp.mean(x * x, axis=-1, keepdims=True) + EPS)) * g


def _norm_kernel(*refs, has_resid, mode):
    it = iter(refs)
    x_ref = next(it)
    x = x_ref[...]
    if has_resid:
        y_ref, gate_ref = next(it), next(it)
        x = x + gate_ref[...] * y_ref[...].astype(F32)
    g_ref = next(it)
    if mode == "final":
        o_ref = next(it)
        o_ref[...] = _rms(x, g_ref[...])
        return
    shift_ref, scale_ref = next(it), next(it)
    if mode == "route":
        wr_ref, br_ref = next(it), next(it)
    if has_resid:
        xo_ref = next(it)
        xo_ref[...] = x
    h_ref = next(it)
    h = _rms(x, g_ref[...]) * (1.0 + scale_ref[...]) + shift_ref[...]
    h_ref[...] = h.astype(BF16)
    if mode == "route":
        idx_ref, gw_ref = next(it), next(it)
        logits = jnp.dot(h, wr_ref[...], preferred_element_type=F32,
                         precision=lax.Precision.HIGHEST) + br_ref[...]
        lane = lax.broadcasted_iota(jnp.int32, logits.shape, 1).astype(F32)
        m1 = jnp.max(logits, axis=-1, keepdims=True)
        i1 = jnp.min(jnp.where(logits == m1, lane, float(LANES)), axis=-1, keepdims=True)
        rest = jnp.where(lane == i1, -jnp.inf, logits)
        m2 = jnp.max(rest, axis=-1, keepdims=True)
        i2 = jnp.min(jnp.where(rest == m2, lane, float(LANES)), axis=-1, keepdims=True)
        e = jnp.exp(m2 - m1)
        den = 1.0 + e
        idx_ref[...] = jnp.where(lane == 0.0, i1, jnp.where(lane == 1.0, i2, 0.0)).astype(jnp.int32)
        gw_ref[...] = jnp.where(lane == 0.0, 1.0 / den, jnp.where(lane == 1.0, e / den, 0.0))


def _norm(x, g, shift=None, scale=None, resid=None, router=None, mode="mod", tm=256):
    m, d = x.shape
    tm = min(tm, m)
    row = pl.BlockSpec((tm, d), lambda i: (i, 0))
    vec = pl.BlockSpec((1, d), lambda i: (0, 0))
    lane_blk = pl.BlockSpec((tm, LANES), lambda i: (i, 0))
    args, in_specs = [x], [row]
    if resid is not None:
        args += [resid[0], resid[1]]
        in_specs += [row, vec]
    args.append(g)
    in_specs.append(vec)
    out_shape, out_specs = [], []
    if mode != "final":
        args += [shift, scale]
        in_specs += [vec, vec]
        if mode == "route":
            args += [router[0], router[1]]
            in_specs += [pl.BlockSpec((d, LANES), lambda i: (0, 0)), pl.BlockSpec((1, LANES), lambda i: (0, 0))]
        if resid is not None:
            out_shape.append(jax.ShapeDtypeStruct((m, d), F32))
            out_specs.append(row)
        out_shape.append(jax.ShapeDtypeStruct((m, d), BF16))
        out_specs.append(row)
        if mode == "route":
            out_shape += [jax.ShapeDtypeStruct((m, LANES), jnp.int32), jax.ShapeDtypeStruct((m, LANES), F32)]
            out_specs += [lane_blk, lane_blk]
    else:
        out_shape.append(jax.ShapeDtypeStruct((m, d), F32))
        out_specs.append(row)
    return pl.pallas_call(
        functools.partial(_norm_kernel, has_resid=resid is not None, mode=mode),
        grid=(m // tm,),
        in_specs=in_specs,
        out_specs=out_specs,
        out_shape=out_shape,
        compiler_params=_params("parallel"),
        name="norm_" + mode,
    )(*args)


def _proj_kernel(a_ref, w_ref, o_ref):
    o_ref[...] = jnp.dot(a_ref[...], w_ref[...].astype(BF16), preferred_element_type=F32).astype(o_ref.dtype)


def _proj(a, w, layer, n_out, col0=0, tm=1024, tn=512, out_dtype=F32, a_buffers=2):
    m, k = a.shape
    tm = min(tm, m)
    cb = col0 // tn
    a_mode = {} if a_buffers == 2 else {"pipeline_mode": pl.Buffered(a_buffers)}
    return pl.pallas_call(
        _proj_kernel,
        grid=(m // tm, n_out // tn),
        in_specs=[
            pl.BlockSpec((tm, k), lambda i, j: (i, 0), **a_mode),
            pl.BlockSpec((None, k, tn), lambda i, j: (layer, 0, j + cb)),
        ],
        out_specs=pl.BlockSpec((tm, tn), lambda i, j: (i, j)),
        out_shape=jax.ShapeDtypeStruct((m, n_out), out_dtype),
        compiler_params=_params("parallel", "arbitrary"),
        name="proj",
    )(a, w)


def _gate_up_kernel(a_ref, wg_ref, wu_ref, o_ref):
    a = a_ref[...]
    g = jnp.dot(a, wg_ref[...].astype(BF16), preferred_element_type=F32)
    u = jnp.dot(a, wu_ref[...].astype(BF16), preferred_element_type=F32)
    o_ref[...] = (_silu(g) * u).astype(o_ref.dtype)


def _gate_up(a, wg, wu, layer, tm=1024, tn=256):
    m, k = a.shape
    n = wg.shape[-1]
    tm = min(tm, m)
    wspec = pl.BlockSpec((None, k, tn), lambda i, j: (layer, 0, j))
    return pl.pallas_call(
        _gate_up_kernel,
        grid=(m // tm, n // tn),
        in_specs=[pl.BlockSpec((tm, k), lambda i, j: (i, 0)), wspec, wspec],
        out_specs=pl.BlockSpec((tm, tn), lambda i, j: (i, j)),
        out_shape=jax.ShapeDtypeStruct((m, n), BF16),
        compiler_params=_params("parallel", "arbitrary"),
        name="gate_up",
    )(a, wg, wu)


def _qkv_kernel(*refs, want_q):
    if want_q:
        q0_ref, q1_ref, kv_ref, cos_ref, sin_ref, gq_ref, gk_ref, qt_ref, k_ref, vt_ref = refs
    else:
        kv_ref, cos_ref, sin_ref, gk_ref, k_ref, vt_ref = refs
    cos, sin = cos_ref[...], sin_ref[...]
    tm = cos.shape[0]

    def head(x, g):
        y = _rms(x, g)
        return y * cos + pltpu.roll(y, HEAD_DIM // 2, axis=1) * sin

    if want_q:
        heads_per_blk = q0_ref.shape[1] // HEAD_DIM
        for b, src in enumerate((q0_ref, q1_ref)):
            for hh in range(heads_per_blk):
                lo = hh * HEAD_DIM
                kv, g = divmod(b * heads_per_blk + hh, KV_GROUP)
                qh = head(src[:, lo:lo + HEAD_DIM], gq_ref[...])
                qt_ref[kv, :, g * tm:(g + 1) * tm] = qh.T.astype(BF16)
    kvw = k_ref.shape[1]
    for hh in range(kvw // HEAD_DIM):
        lo = hh * HEAD_DIM
        k_ref[:, lo:lo + HEAD_DIM] = head(kv_ref[:, lo:lo + HEAD_DIM], gk_ref[...]).astype(BF16)
    vt_ref[...] = kv_ref[:, kvw:].T.astype(BF16)


def _qkv_prep(p, cosf, sinf, gq_scaled, gk, q_col0, kv_col0, tm=512):
    m = p.shape[0]
    tm = min(tm, m)
    blk = 2 * KV_GROUP * HEAD_DIM
    kvw = blk // 2
    hkv = kvw // HEAD_DIM
    want_q = q_col0 is not None
    pspec = lambda c: pl.BlockSpec((tm, blk), lambda i: (i, c))
    tab = pl.BlockSpec((tm, HEAD_DIM), lambda i: (i, 0))
    gvec = pl.BlockSpec((1, HEAD_DIM), lambda i: (0, 0))
    kv_specs = [pl.BlockSpec((tm, kvw), lambda i: (i, 0)), pl.BlockSpec((kvw, tm), lambda i: (0, i))]
    kv_shapes = [jax.ShapeDtypeStruct((m, kvw), BF16), jax.ShapeDtypeStruct((kvw, m), BF16)]
    if want_q:
        args = (p, p, p, cosf, sinf, gq_scaled, gk)
        in_specs = [pspec(q_col0 // blk), pspec(q_col0 // blk + 1), pspec(kv_col0 // blk), tab, tab, gvec, gvec]
        out_specs = [pl.BlockSpec((hkv, None, HEAD_DIM, KV_GROUP * tm), lambda i: (0, i, 0, 0))] + kv_specs
        out_shape = [jax.ShapeDtypeStruct((hkv, m // tm, HEAD_DIM, KV_GROUP * tm), BF16)] + kv_shapes
    else:
        args = (p, cosf, sinf, gk)
        in_specs = [pspec(kv_col0 // blk), tab, tab, gvec]
        out_specs = kv_specs
        out_shape = kv_shapes
    return pl.pallas_call(
        functools.partial(_qkv_kernel, want_q=want_q),
        grid=(m // tm,),
        in_specs=in_specs,
        out_specs=out_specs,
        out_shape=out_shape,
        compiler_params=_params("parallel"),
        name="qkv_prep",
    )(*args)


def _flash_t_kernel(qn_ref, kn_ref, qc_ref, kc_ref, vt_ref, o_ref,
                    s_a, s_b, mx_a, mx_b, m_ref, l_ref, acc_ref):
    h, i, j = pl.program_id(0), pl.program_id(1), pl.program_id(2)
    nq, nk = pl.num_programs(1), pl.num_programs(2)
    step = (h * nq + i) * nk + j

    def scores(k_ref, q_ref, s_ref, mx_ref):
        s = jnp.dot(k_ref[...], q_ref[...], preferred_element_type=F32)
        s_ref[...] = s
        mx_ref[...] = jnp.max(s, axis=0, keepdims=True)

    @pl.when(step == 0)
    def _():
        scores(kc_ref, qc_ref, s_a, mx_a)

    @pl.when(j == 0)
    def _():
        m_ref[...] = jnp.full(m_ref.shape, -jnp.inf, F32)
        l_ref[...] = jnp.zeros(l_ref.shape, F32)
        acc_ref[...] = jnp.zeros(acc_ref.shape, F32)

    def body(s_cur, mx_cur, s_nxt, mx_nxt):
        scores(kn_ref, qn_ref, s_nxt, mx_nxt)
        m_prev = m_ref[...]
        m_new = jnp.maximum(m_prev, mx_cur[...])
        alpha = jnp.exp2(m_prev - m_new)
        p = jnp.exp2(s_cur[...] - m_new)
        l_ref[...] = alpha * l_ref[...] + jnp.sum(p, axis=0, keepdims=True)
        acc_ref[...] = alpha * acc_ref[...] + jnp.dot(vt_ref[...], p.astype(BF16), preferred_element_type=F32)
        m_ref[...] = m_new

    @pl.when(step % 2 == 0)
    def _():
        body(s_a, mx_a, s_b, mx_b)

    @pl.when(step % 2 == 1)
    def _():
        body(s_b, mx_b, s_a, mx_a)

    @pl.when(j == nk - 1)
    def _():
        tq = o_ref.shape[0]
        o = acc_ref[...] / l_ref[...]
        for g in range(KV_GROUP):
            o_ref[:, g * HEAD_DIM:(g + 1) * HEAD_DIM] = o[:, g * tq:(g + 1) * tq].T.astype(o_ref.dtype)


def _flash_t(qt, k, vt, tk=1408):
    hkv, nq, _, nqc = qt.shape
    tq = nqc // KV_GROUP
    lk = k.shape[0]
    tk = min(tk, lk)
    nk = lk // tk
    gw = KV_GROUP * HEAD_DIM

    def nxt(h, i, j):
        j1 = j + 1
        i1 = i + j1 // nk
        h1 = h + i1 // nq
        return jnp.minimum(h1, hkv - 1), i1 % nq, j1 % nk

    def qn_map(h, i, j):
        h1, i1, _ = nxt(h, i, j)
        return h1, i1, 0, 0

    def kn_map(h, i, j):
        h1, _, j1 = nxt(h, i, j)
        return j1, h1

    qspec = lambda m: pl.BlockSpec((None, None, HEAD_DIM, nqc), m)
    return pl.pallas_call(
        _flash_t_kernel,
        grid=(hkv, nq, nk),
        in_specs=[
            qspec(qn_map),
            pl.BlockSpec((tk, HEAD_DIM), kn_map),
            qspec(lambda h, i, j: (h, i, 0, 0)),
            pl.BlockSpec((tk, HEAD_DIM), lambda h, i, j: (0, h)),
            pl.BlockSpec((HEAD_DIM, tk), lambda h, i, j: (h, j)),
        ],
        out_specs=pl.BlockSpec((tq, gw), lambda h, i, j: (i, h)),
        out_shape=jax.ShapeDtypeStruct((nq * tq, hkv * gw), BF16),
        scratch_shapes=[pltpu.VMEM((tk, nqc), F32), pltpu.VMEM((tk, nqc), F32),
                        pltpu.VMEM((1, nqc), F32), pltpu.VMEM((1, nqc), F32),
                        pltpu.VMEM((1, nqc), F32), pltpu.VMEM((1, nqc), F32),
                        pltpu.VMEM((HEAD_DIM, nqc), F32)],
        compiler_params=_params("arbitrary", "arbitrary", "arbitrary"),
        name="flash_t",
    )(qt, k, qt, k, vt)


def _mixer_kernel(pc_ref, pp_ref, pn_ref, cb_ref, cc_ref, ccp_ref, ccn_ref, cx_ref, cxp_ref, cxn_ref,
                  ya_ref, wp_ref, ps_ref, cw_ref, o_ref, *, seq_len):
    i = pl.program_id(0)
    last = pl.num_programs(0) - 1
    tm = pc_ref.shape[0]
    n_ext = tm + 2 * POOL_HALO
    pool_w = pc_ref.shape[1]
    grp = pool_w // len(POOL_WINDOWS)

    prev = jnp.where(i == 0, 0.0, pp_ref[...])
    nxt = jnp.where(i == last, 0.0, pn_ref[...])
    ext = jnp.concatenate([prev, pc_ref[...], nxt], axis=0)
    t = (i * tm + lax.broadcasted_iota(jnp.int32, (tm, 1), 0))

    def shift_up(x, k):
        return pltpu.roll(x, n_ext - k, axis=0)

    def shift_down(x, k):
        return pltpu.roll(x, k, axis=0)

    for gi, w in enumerate(POOL_WINDOWS):
        e = ext[:, gi * grp:(gi + 1) * grp]
        half = w // 2
        part, span = e, 1
        while span < half:
            part = part + shift_up(part, span)
            span *= 2
        win = shift_down(part, half) + part
        win = win[POOL_HALO:POOL_HALO + tm]
        lo = jnp.clip(t - half, 0, seq_len)
        hi = jnp.clip(t - half + w, 0, seq_len)
        cnt = (hi - lo).astype(F32)
        pooled = win / cnt - e[POOL_HALO:POOL_HALO + tm]
        y = jnp.dot(pooled.astype(BF16), wp_ref[gi].astype(BF16), preferred_element_type=F32)
        o_ref[:, gi * grp:(gi + 1) * grp] = (y * ps_ref[:, gi * grp:(gi + 1) * grp]).astype(o_ref.dtype)

    aw = ya_ref.shape[1]
    o_ref[:, pool_w:pool_w + aw] = ya_ref[...]

    u = cc_ref[...] * cx_ref[...]
    u_before = jnp.where(i == 0, 0.0, ccp_ref[CONV_HALO - 1:CONV_HALO, :] * cxp_ref[CONV_HALO - 1:CONV_HALO, :])
    u_after = jnp.where(i == last, 0.0, ccn_ref[0:1, :] * cxn_ref[0:1, :])
    r = lax.broadcasted_iota(jnp.int32, (tm, 1), 0)
    um1 = jnp.where(r == 0, u_before, pltpu.roll(u, 1, axis=0))
    up1 = jnp.where(r == tm - 1, u_after, pltpu.roll(u, tm - 1, axis=0))
    conv = cw_ref[0:1, :] * um1 + cw_ref[1:2, :] * u + cw_ref[2:3, :] * up1
    o_ref[:, pool_w + aw:] = (cb_ref[...] * conv).astype(o_ref.dtype)


def _mixer(p, ya, w_pool, pool_scale, conv_w, cb_col0, tm=256):
    m = p.shape[0]
    tm = min(tm, m)
    blk = pool_scale.shape[-1]
    c0 = cb_col0 // blk
    nblk_pool, nblk_conv = m // POOL_HALO, m // CONV_HALO
    rp, rc = tm // POOL_HALO, tm // CONV_HALO

    cur = lambda c: pl.BlockSpec((tm, blk), lambda i: (i, c))
    prev_p = pl.BlockSpec((POOL_HALO, blk), lambda i: (jnp.maximum(i * rp - 1, 0), 0))
    next_p = pl.BlockSpec((POOL_HALO, blk), lambda i: (jnp.minimum((i + 1) * rp, nblk_pool - 1), 0))
    prev_c = lambda c: pl.BlockSpec((CONV_HALO, blk), lambda i: (jnp.maximum(i * rc - 1, 0), c))
    next_c = lambda c: pl.BlockSpec((CONV_HALO, blk), lambda i: (jnp.minimum((i + 1) * rc, nblk_conv - 1), c))
    aw = ya.shape[1]
    full = lambda a: pl.BlockSpec(a.shape, lambda i: (0,) * a.ndim)
    ps = pool_scale.reshape(1, blk)
    return pl.pallas_call(
        functools.partial(_mixer_kernel, seq_len=m),
        grid=(m // tm,),
        in_specs=[cur(0), prev_p, next_p,
                  cur(c0), cur(c0 + 1), prev_c(c0 + 1), next_c(c0 + 1),
                  cur(c0 + 2), prev_c(c0 + 2), next_c(c0 + 2),
                  pl.BlockSpec((tm, aw), lambda i: (i, 0)),
                  full(w_pool), full(ps), full(conv_w)],
        out_specs=pl.BlockSpec((tm, 2 * blk + aw), lambda i: (i, 0)),
        out_shape=jax.ShapeDtypeStruct((m, 2 * blk + aw), BF16),
        compiler_params=_params("parallel"),
        name="mixer",
    )(p, p, p, p, p, p, p, p, p, p, ya, w_pool, ps, conv_w)


def _down(act, wd, layer, tm=1024, tn=256):
    return _proj(act, wd, layer, wd.shape[-1], tm=tm, tn=tn, a_buffers=1)


def _moe_tile_kernel(te_ref, nv_ref, tr_ref, a_ref, *refs, body):
    w_refs, o_ref = refs[:-1], refs[-1]
    tm = o_ref.shape[0]
    half = tm // 2
    rows = tr_ref[pl.program_id(1)]

    @pl.when(rows > half)
    def _():
        body(a_ref, *w_refs, o_ref)

    @pl.when(jnp.logical_and(rows > 0, rows <= half))
    def _():
        body(a_ref.at[pl.ds(0, half)], *w_refs, o_ref.at[pl.ds(0, half)])
        o_ref[pl.ds(half, tm - half), :] = jnp.zeros((tm - half, o_ref.shape[1]), o_ref.dtype)

    @pl.when(rows == 0)
    def _():
        o_ref[...] = jnp.zeros(o_ref.shape, o_ref.dtype)


_moe_gate_up_kernel = functools.partial(_moe_tile_kernel, body=_gate_up_kernel)
_moe_down_kernel = functools.partial(_moe_tile_kernel, body=_proj_kernel)


def _moe_call(kernel, a, weights, tile_expert, n_valid, tile_rows, n_out, out_dtype, tm, tn, name):
    r, k = a.shape
    n_tiles = r // tm
    row = lambda j, t, te, nv, tr: (jnp.minimum(t, nv[0] - 1), 0)
    wspec = pl.BlockSpec((None, k, tn), lambda j, t, te, nv, tr: (te[t], 0, j))
    return pl.pallas_call(
        kernel,
        grid_spec=pltpu.PrefetchScalarGridSpec(
            num_scalar_prefetch=3,
            grid=(n_out // tn, n_tiles),
            in_specs=[pl.BlockSpec((tm, k), row)] + [wspec] * len(weights),
            out_specs=pl.BlockSpec((tm, tn), lambda j, t, te, nv, tr: (t, j)),
        ),
        out_shape=jax.ShapeDtypeStruct((r, n_out), out_dtype),
        compiler_params=_params("parallel", "arbitrary"),
        name=name,
    )(tile_expert, n_valid, tile_rows, a, *weights)


def _moe_ffn(h, ridx, wg, wu, wd, tm=512, tn=512, tn_down=1024):
    m, d = h.shape
    n_slots = m * TOP_K
    r_max = n_slots + N_EXPERTS * tm
    n_tiles = r_max // tm
    e_flat = ridx[:, :TOP_K].reshape(n_slots)
    order = jnp.argsort(e_flat, stable=True).astype(jnp.int32)
    counts = jnp.sum((e_flat[:, None] == jnp.arange(N_EXPERTS)[None, :]).astype(jnp.int32), axis=0)
    padded = ((counts + tm - 1) // tm) * tm
    gend = jnp.cumsum(padded)
    gstart = gend - padded
    cstart = jnp.cumsum(counts) - counts
    n_valid = (gend[-1] // tm).astype(jnp.int32).reshape(1)
    tile_expert = jnp.minimum(jnp.searchsorted(gend, jnp.arange(n_tiles) * tm, side="right"),
                              N_EXPERTS - 1).astype(jnp.int32)
    last_e = tile_expert[jnp.maximum(n_valid[0] - 1, 0)]
    tile_expert = jnp.where(jnp.arange(n_tiles) < n_valid[0], tile_expert, last_e)
    rows = jnp.arange(r_max)
    row_e = tile_expert[rows // tm]
    off = rows - gstart[row_e]
    row_valid = (off < counts[row_e]) & (rows < gend[-1])
    src = jnp.clip(cstart[row_e] + off, 0, n_slots - 1)
    row_token = jnp.where(row_valid, order[src] // TOP_K, rows % m)
    tile_lo = jnp.arange(n_tiles) * tm
    tile_rows = jnp.clip(gstart[tile_expert] + counts[tile_expert] - tile_lo, 0, tm)
    tile_rows = jnp.where(jnp.arange(n_tiles) < n_valid[0], tile_rows, 0).astype(jnp.int32)
    rank = jnp.argsort(order).astype(jnp.int32)
    slot_row = (gstart[e_flat] + rank - cstart[e_flat]).reshape(m, TOP_K)

    hs = h.at[row_token].get(mode="promise_in_bounds")
    act = _moe_call(_moe_gate_up_kernel, hs, (wg, wu), tile_expert, n_valid, tile_rows, wg.shape[-1], BF16,
                    tm, tn, "moe_gate_up")
    ys = _moe_call(_moe_down_kernel, act, (wd,), tile_expert, n_valid, tile_rows, d, BF16, tm, tn_down,
                   "moe_down")
    return ys, slot_row


def _combine_kernel(x_ref, y1_ref, y2_ref, gw_ref, gate_ref, g_ref, o_ref):
    gw = gw_ref[...]
    y = gw[:, 0:1] * y1_ref[...].astype(F32) + gw[:, 1:2] * y2_ref[...].astype(F32)
    o_ref[...] = _rms(x_ref[...] + gate_ref[...] * y, g_ref[...])


def _combine_final(x, y1, y2, gw, gate, g, tm=256):
    m, d = x.shape
    row = pl.BlockSpec((tm, d), lambda i: (i, 0))
    vec = pl.BlockSpec((1, d), lambda i: (0, 0))
    return pl.pallas_call(
        _combine_kernel,
        grid=(m // tm,),
        in_specs=[row, row, row, pl.BlockSpec((tm, LANES), lambda i: (i, 0)), vec, vec],
        out_specs=row,
        out_shape=jax.ShapeDtypeStruct((m, d), F32),
        compiler_params=_params("parallel"),
        name="combine_final",
    )(x, y1, y2, gw, gate, g)


def _rope_tables(seq_len):
    rows = seq_len // GRID_W
    row = jnp.repeat(jnp.arange(rows), GRID_W).astype(F32)
    col = jnp.tile(jnp.arange(GRID_W), rows).astype(F32)
    n_axis = HEAD_DIM // 4
    inv = ROPE_BASE ** (-jnp.arange(n_axis, dtype=F32) / n_axis)
    ang = jnp.concatenate([row[:, None] * inv, col[:, None] * inv], axis=-1)
    cos, sin = jnp.cos(ang), jnp.sin(ang)
    return jnp.concatenate([cos, cos], axis=-1), jnp.concatenate([-sin, sin], axis=-1)


def kernel(x, c, ctx, c_ctx, w_mod, b_mod, g_mix, w_in, w_pool, pool_scale, g_q, g_k, conv_w, w_out, g_ffn,
           w_gate_dense, w_up_dense, w_down_dense, w_router, b_router, w_gate_exp, w_up_exp, w_down_exp, g_final):
    batch, seq, d = x.shape
    depth = w_mod.shape[0]
    assert batch == 1 and depth == 2
    n_ctx = ctx.shape[1]
    pool_w = pool_scale.shape[-1]
    conv_wd = conv_w.shape[-1]
    q_w = (d // (2 * HEAD_DIM)) * HEAD_DIM
    kv_w = q_w // KV_GROUP
    q_off = pool_w
    k_off = q_off + q_w
    cb_off = k_off + 2 * kv_w
    in_w = cb_off + 3 * conv_wd

    xs, cs = x[0], ctx[0]
    cvec = jnp.zeros((SUBLANES, d), F32).at[0].set(c[0]).at[1].set(c_ctx)
    mods = _mods(cvec, w_mod, b_mod)

    def mod(l, row, k):
        return mods[l, row:row + 1, k * d:(k + 1) * d]

    cosf, sinf = _rope_tables(seq)
    cos_c, sin_c = jnp.ones((n_ctx, HEAD_DIM), F32), jnp.zeros((n_ctx, HEAD_DIM), F32)
    qscale = HEAD_DIM ** -0.5 * LOG2E

    pend_x = pend_c = None
    for l in range(depth):
        last = l == depth - 1
        vec = lambda a: a[l].reshape(1, -1)
        gq, gk = vec(g_q) * qscale, vec(g_k)

        def first_norm(stream, pend, row):
            if pend is None:
                (h,) = _norm(stream, vec(g_mix), mod(l, row, 0), mod(l, row, 1))
                return stream, h
            return _norm(stream, vec(g_mix), mod(l, row, 0), mod(l, row, 1), resid=pend)

        cs, hc = first_norm(cs, pend_c, 1)
        if last:
            pkv = _proj(hc, w_in, l, 2 * kv_w, col0=k_off)
            kc, vtc = _qkv_prep(pkv, cos_c, sin_c, gq, gk, None, 0)
        else:
            pc = _proj(hc, w_in, l, in_w)
            qtc, kc, vtc = _qkv_prep(pc, cos_c, sin_c, gq, gk, q_off, k_off)
            yac = _flash_t(qtc, kc, vtc)
            mixc = _mixer(pc, yac, w_pool[l], pool_scale[l], conv_w[l], cb_off)
            ymc = _proj(mixc, w_out, l, d)

        xs, hx = first_norm(xs, pend_x, 0)
        px = _proj(hx, w_in, l, in_w)
        qtx, kx, vtx = _qkv_prep(px, cosf, sinf, gq, gk, q_off, k_off)
        k_all = jnp.concatenate([kc, kx], axis=0)
        vt_all = jnp.concatenate([vtc, vtx], axis=1)
        yax = _flash_t(qtx, k_all, vt_all)
        mixx = _mixer(px, yax, w_pool[l], pool_scale[l], conv_w[l], cb_off)
        ymx = _proj(mixx, w_out, l, d)

        i = l // 2
        if l % 2 == 0:
            def ffn(stream, ym, row):
                xn, h = _norm(stream, vec(g_ffn), mod(l, row, 3), mod(l, row, 4), resid=(ym, mod(l, row, 2)))
                act = _gate_up(h, w_gate_dense, w_up_dense, i)
                return xn, _down(act, w_down_dense, i)

            xs, yfx = ffn(xs, ymx, 0)
            pend_x = (yfx, mod(l, 0, 5))
            if not last:
                cs, yfc = ffn(cs, ymc, 1)
                pend_c = (yfc, mod(l, 1, 5))
        else:
            wr = jnp.zeros((d, LANES), F32).at[:, :N_EXPERTS].set(w_router[i])
            br = jnp.full((1, LANES), NEG_BIG, F32).at[0, :N_EXPERTS].set(b_router[i])
            xs, hf, ridx, rgw = _norm(xs, vec(g_ffn), mod(l, 0, 3), mod(l, 0, 4), resid=(ymx, mod(l, 0, 2)),
                                      router=(wr, br), mode="route")
            ys, slot_row = _moe_ffn(hf, ridx, w_gate_exp[i], w_up_exp[i], w_down_exp[i])
            y1 = ys.at[slot_row[:, 0]].get(mode="promise_in_bounds")
            y2 = ys.at[slot_row[:, 1]].get(mode="promise_in_bounds")
            pend_x = (y1, y2, rgw, mod(l, 0, 5))

    if len(pend_x) == 4:
        out = _combine_final(xs, pend_x[0], pend_x[1], pend_x[2], pend_x[3], g_final.reshape(1, -1))
    else:
        (out,) = _norm(xs, g_final.reshape(1, -1), resid=pend_x, mode="final")
    return out[None]
```

```python
import functools

import jax
import jax.numpy as jnp
from jax import lax
from jax.experimental import pallas as pl
from jax.experimental.pallas import tpu as pltpu

F32 = jnp.float32
BF16 = jnp.bfloat16

POOL_WINDOWS = (2, 4, 8, 16)
HEAD_DIM = 128
KV_GROUP = 4
N_MOD = 6
N_EXPERTS = 8
TOP_K = 2
GRID_W = 64
ROPE_BASE = 10000.0
EPS = 1e-6

LANES = 128
SUBLANES = 8
VMEM_LIMIT_BYTES = 56 * 1024 * 1024

POOL_HALO = 16
CONV_HALO = 8
NEG_BIG = -1e30
LOG2E = 1.4426950408889634


def _params(*sem):
    return pltpu.CompilerParams(dimension_semantics=sem, vmem_limit_bytes=VMEM_LIMIT_BYTES)


def _silu(x):
    return x * jax.nn.sigmoid(x)


def _mods_kernel(c_ref, w_ref, b_ref, o_ref):
    s = _silu(c_ref[...]).astype(BF16)
    w = w_ref[...].astype(BF16)
    o_ref[...] = jnp.dot(s, w, preferred_element_type=F32) + b_ref[...]


def _mods(cvec, w_mod, b_mod, tn=512):
    depth, d, n = w_mod.shape
    return pl.pallas_call(
        _mods_kernel,
        grid=(depth, n // tn),
        in_specs=[
            pl.BlockSpec((SUBLANES, d), lambda l, j: (0, 0)),
            pl.BlockSpec((None, d, tn), lambda l, j: (l, 0, j)),
            pl.BlockSpec((None, 1, tn), lambda l, j: (l, 0, j)),
        ],
        out_specs=pl.BlockSpec((None, SUBLANES, tn), lambda l, j: (l, 0, j)),
        out_shape=jax.ShapeDtypeStruct((depth, SUBLANES, n), F32),
        compiler_params=_params("parallel", "parallel"),
        name="mods",
    )(cvec, w_mod, b_mod.reshape(depth, 1, n))


def _rms(x, g):
    return (x * lax.rsqrt(jnp.mean(x * x, axis=-1, keepdims=True) + EPS)) * g


def _norm_kernel(*refs, has_resid, mode):
    it = iter(refs)
    x_ref = next(it)
    x = x_ref[...]
    if has_resid:
        y_ref, gate_ref = next(it), next(it)
        x = x + gate_ref[...] * y_ref[...].astype(F32)
    g_ref = next(it)
    if mode == "final":
        o_ref = next(it)
        o_ref[...] = _rms(x, g_ref[...])
        return
    shift_ref, scale_ref = next(it), next(it)
    if mode == "route":
        wr_ref, br_ref = next(it), next(it)
    if has_resid:
        xo_ref = next(it)
        xo_ref[...] = x
    h_ref = next(it)
    h = _rms(x, g_ref[...]) * (1.0 + scale_ref[...]) + shift_ref[...]
    h_ref[...] = h.astype(BF16)
    if mode == "route":
        idx_ref, gw_ref = next(it), next(it)
        logits = jnp.dot(h, wr_ref[...], preferred_element_type=F32,
                         precision=lax.Precision.HIGHEST) + br_ref[...]
        lane = lax.broadcasted_iota(jnp.int32, logits.shape, 1).astype(F32)
        m1 = jnp.max(logits, axis=-1, keepdims=True)
        i1 = jnp.min(jnp.where(logits == m1, lane, float(LANES)), axis=-1, keepdims=True)
        rest = jnp.where(lane == i1, -jnp.inf, logits)
        m2 = jnp.max(rest, axis=-1, keepdims=True)
        i2 = jnp.min(jnp.where(rest == m2, lane, float(LANES)), axis=-1, keepdims=True)
        e = jnp.exp(m2 - m1)
        den = 1.0 + e
        idx_ref[...] = jnp.where(lane == 0.0, i1, jnp.where(lane == 1.0, i2, 0.0)).astype(jnp.int32)
        gw_ref[...] = jnp.where(lane == 0.0, 1.0 / den, jnp.where(lane == 1.0, e / den, 0.0))


def _norm(x, g, shift=None, scale=None, resid=None, router=None, mode="mod", tm=256):
    m, d = x.shape
    tm = min(tm, m)
    row = pl.BlockSpec((tm, d), lambda i: (i, 0))
    vec = pl.BlockSpec((1, d), lambda i: (0, 0))
    lane_blk = pl.BlockSpec((tm, LANES), lambda i: (i, 0))
    args, in_specs = [x], [row]
    if resid is not None:
        args += [resid[0], resid[1]]
        in_specs += [row, vec]
    args.append(g)
    in_specs.append(vec)
    out_shape, out_specs = [], []
    if mode != "final":
        args += [shift, scale]
        in_specs += [vec, vec]
        if mode == "route":
            args += [router[0], router[1]]
            in_specs += [pl.BlockSpec((d, LANES), lambda i: (0, 0)), pl.BlockSpec((1, LANES), lambda i: (0, 0))]
        if resid is not None:
            out_shape.append(jax.ShapeDtypeStruct((m, d), F32))
            out_specs.append(row)
        out_shape.append(jax.ShapeDtypeStruct((m, d), BF16))
        out_specs.append(row)
        if mode == "route":
            out_shape += [jax.ShapeDtypeStruct((m, LANES), jnp.int32), jax.ShapeDtypeStruct((m, LANES), F32)]
            out_specs += [lane_blk, lane_blk]
    else:
        out_shape.append(jax.ShapeDtypeStruct((m, d), F32))
        out_specs.append(row)
    return pl.pallas_call(
        functools.partial(_norm_kernel, has_resid=resid is not None, mode=mode),
        grid=(m // tm,),
        in_specs=in_specs,
        out_specs=out_specs,
        out_shape=out_shape,
        compiler_params=_params("parallel"),
        name="norm_" + mode,
    )(*args)


def _proj_kernel(a_ref, w_ref, o_ref):
    o_ref[...] = jnp.dot(a_ref[...], w_ref[...].astype(BF16), preferred_element_type=F32).astype(o_ref.dtype)


def _proj(a, w, layer, n_out, col0=0, tm=1024, tn=512, out_dtype=F32, a_buffers=2):
    m, k = a.shape
    tm = min(tm, m)
    cb = col0 // tn
    a_mode = {} if a_buffers == 2 else {"pipeline_mode": pl.Buffered(a_buffers)}
    return pl.pallas_call(
        _proj_kernel,
        grid=(m // tm, n_out // tn),
        in_specs=[
            pl.BlockSpec((tm, k), lambda i, j: (i, 0), **a_mode),
            pl.BlockSpec((None, k, tn), lambda i, j: (layer, 0, j + cb)),
        ],
        out_specs=pl.BlockSpec((tm, tn), lambda i, j: (i, j)),
        out_shape=jax.ShapeDtypeStruct((m, n_out), out_dtype),
        compiler_params=_params("parallel", "arbitrary"),
        name="proj",
    )(a, w)


def _gate_up_kernel(a_ref, wg_ref, wu_ref, o_ref):
    a = a_ref[...]
    g = jnp.dot(a, wg_ref[...].astype(BF16), preferred_element_type=F32)
    u = jnp.dot(a, wu_ref[...].astype(BF16), preferred_element_type=F32)
    o_ref[...] = (_silu(g) * u).astype(o_ref.dtype)


def _gate_up(a, wg, wu, layer, tm=1024, tn=256):
    m, k = a.shape
    n = wg.shape[-1]
    tm = min(tm, m)
    wspec = pl.BlockSpec((None, k, tn), lambda i, j: (layer, 0, j))
    return pl.pallas_call(
        _gate_up_kernel,
        grid=(m // tm, n // tn),
        in_specs=[pl.BlockSpec((tm, k), lambda i, j: (i, 0)), wspec, wspec],
        out_specs=pl.BlockSpec((tm, tn), lambda i, j: (i, j)),
        out_shape=jax.ShapeDtypeStruct((m, n), BF16),
        compiler_params=_params("parallel", "arbitrary"),
        name="gate_up",
    )(a, wg, wu)


def _qkv_kernel(*refs, want_q):
    if want_q:
        q0_ref, q1_ref, kv_ref, cos_ref, sin_ref, gq_ref, gk_ref, qt_ref, k_ref, vt_ref = refs
    else:
        kv_ref, cos_ref, sin_ref, gk_ref, k_ref, vt_ref = refs
    cos, sin = cos_ref[...], sin_ref[...]
    tm = cos.shape[0]

    def head(x, g):
        y = _rms(x, g)
        return y * cos + pltpu.roll(y, HEAD_DIM // 2, axis=1) * sin

    if want_q:
        heads_per_blk = q0_ref.shape[1] // HEAD_DIM
        for b, src in enumerate((q0_ref, q1_ref)):
            for hh in range(heads_per_blk):
                lo = hh * HEAD_DIM
                kv, g = divmod(b * heads_per_blk + hh, KV_GROUP)
                qh = head(src[:, lo:lo + HEAD_DIM], gq_ref[...])
                qt_ref[kv, :, g * tm:(g + 1) * tm] = qh.T.astype(BF16)
    kvw = k_ref.shape[1]
    for hh in range(kvw // HEAD_DIM):
        lo = hh * HEAD_DIM
        k_ref[:, lo:lo + HEAD_DIM] = head(kv_ref[:, lo:lo + HEAD_DIM], gk_ref[...]).astype(BF16)
    vt_ref[...] = kv_ref[:, kvw:].T.astype(BF16)


def _qkv_prep(p, cosf, sinf, gq_scaled, gk, q_col0, kv_col0, tm=512):
    m = p.shape[0]
    tm = min(tm, m)
    blk = 2 * KV_GROUP * HEAD_DIM
    kvw = blk // 2
    hkv = kvw // HEAD_DIM
    want_q = q_col0 is not None
    pspec = lambda c: pl.BlockSpec((tm, blk), lambda i: (i, c))
    tab = pl.BlockSpec((tm, HEAD_DIM), lambda i: (i, 0))
    gvec = pl.BlockSpec((1, HEAD_DIM), lambda i: (0, 0))
    kv_specs = [pl.BlockSpec((tm, kvw), lambda i: (i, 0)), pl.BlockSpec((kvw, tm), lambda i: (0, i))]
    kv_shapes = [jax.ShapeDtypeStruct((m, kvw), BF16), jax.ShapeDtypeStruct((kvw, m), BF16)]
    if want_q:
        args = (p, p, p, cosf, sinf, gq_scaled, gk)
        in_specs = [pspec(q_col0 // blk), pspec(q_col0 // blk + 1), pspec(kv_col0 // blk), tab, tab, gvec, gvec]
        out_specs = [pl.BlockSpec((hkv, None, HEAD_DIM, KV_GROUP * tm), lambda i: (0, i, 0, 0))] + kv_specs
        out_shape = [jax.ShapeDtypeStruct((hkv, m // tm, HEAD_DIM, KV_GROUP * tm), BF16)] + kv_shapes
    else:
        args = (p, cosf, sinf, gk)
        in_specs = [pspec(kv_col0 // blk), tab, tab, gvec]
        out_specs = kv_specs
        out_shape = kv_shapes
    return pl.pallas_call(
        functools.partial(_qkv_kernel, want_q=want_q),
        grid=(m // tm,),
        in_specs=in_specs,
        out_specs=out_specs,
        out_shape=out_shape,
        compiler_params=_params("parallel"),
        name="qkv_prep",
    )(*args)


def _flash_t_kernel(qn_ref, kn_ref, qc_ref, kc_ref, vt_ref, o_ref,
                    s_a, s_b, mx_a, mx_b, m_ref, l_ref, acc_ref):
    h, i, j = pl.program_id(0), pl.program_id(1), pl.program_id(2)
    nq, nk = pl.num_programs(1), pl.num_programs(2)
    step = (h * nq + i) * nk + j

    def scores(k_ref, q_ref, s_ref, mx_ref):
        s = jnp.dot(k_ref[...], q_ref[...], preferred_element_type=F32)
        s_ref[...] = s
        mx_ref[...] = jnp.max(s, axis=0, keepdims=True)

    @pl.when(step == 0)
    def _():
        scores(kc_ref, qc_ref, s_a, mx_a)

    @pl.when(j == 0)
    def _():
        m_ref[...] = jnp.full(m_ref.shape, -jnp.inf, F32)
        l_ref[...] = jnp.zeros(l_ref.shape, F32)
        acc_ref[...] = jnp.zeros(acc_ref.shape, F32)

    def body(s_cur, mx_cur, s_nxt, mx_nxt):
        scores(kn_ref, qn_ref, s_nxt, mx_nxt)
        m_prev = m_ref[...]
        m_new = jnp.maximum(m_prev, mx_cur[...])
        alpha = jnp.exp2(m_prev - m_new)
        p = jnp.exp2(s_cur[...] - m_new)
        l_ref[...] = alpha * l_ref[...] + jnp.sum(p, axis=0, keepdims=True)
        acc_ref[...] = alpha * acc_ref[...] + jnp.dot(vt_ref[...], p.astype(BF16), preferred_element_type=F32)
        m_ref[...] = m_new

    @pl.when(step % 2 == 0)
    def _():
        body(s_a, mx_a, s_b, mx_b)

    @pl.when(step % 2 == 1)
    def _():
        body(s_b, mx_b, s_a, mx_a)

    @pl.when(j == nk - 1)
    def _():
        tq = o_ref.shape[0]
        o = acc_ref[...] / l_ref[...]
        for g in range(KV_GROUP):
            o_ref[:, g * HEAD_DIM:(g + 1) * HEAD_DIM] = o[:, g * tq:(g + 1) * tq].T.astype(o_ref.dtype)


def _flash_t(qt, k, vt, tk=1408):
    hkv, nq, _, nqc = qt.shape
    tq = nqc // KV_GROUP
    lk = k.shape[0]
    tk = min(tk, lk)
    nk = lk // tk
    gw = KV_GROUP * HEAD_DIM

    def nxt(h, i, j):
        j1 = j + 1
        i1 = i + j1 // nk
        h1 = h + i1 // nq
        return jnp.minimum(h1, hkv - 1), i1 % nq, j1 % nk

    def qn_map(h, i, j):
        h1, i1, _ = nxt(h, i, j)
        return h1, i1, 0, 0

    def kn_map(h, i, j):
        h1, _, j1 = nxt(h, i, j)
        return j1, h1

    qspec = lambda m: pl.BlockSpec((None, None, HEAD_DIM, nqc), m)
    return pl.pallas_call(
        _flash_t_kernel,
        grid=(hkv, nq, nk),
        in_specs=[
            qspec(qn_map),
            pl.BlockSpec((tk, HEAD_DIM), kn_map),
            qspec(lambda h, i, j: (h, i, 0, 0)),
            pl.BlockSpec((tk, HEAD_DIM), lambda h, i, j: (0, h)),
            pl.BlockSpec((HEAD_DIM, tk), lambda h, i, j: (h, j)),
        ],
        out_specs=pl.BlockSpec((tq, gw), lambda h, i, j: (i, h)),
        out_shape=jax.ShapeDtypeStruct((nq * tq, hkv * gw), BF16),
        scratch_shapes=[pltpu.VMEM((tk, nqc), F32), pltpu.VMEM((tk, nqc), F32),
                        pltpu.VMEM((1, nqc), F32), pltpu.VMEM((1, nqc), F32),
                        pltpu.VMEM((1, nqc), F32), pltpu.VMEM((1, nqc), F32),
                        pltpu.VMEM((HEAD_DIM, nqc), F32)],
        compiler_params=_params("arbitrary", "arbitrary", "arbitrary"),
        name="flash_t",
    )(qt, k, qt, k, vt)


def _mixer_kernel(pc_ref, pp_ref, pn_ref, cb_ref, cc_ref, ccp_ref, ccn_ref, cx_ref, cxp_ref, cxn_ref,
                  ya_ref, wp_ref, ps_ref, cw_ref, o_ref, *, seq_len):
    i = pl.program_id(0)
    last = pl.num_programs(0) - 1
    tm = pc_ref.shape[0]
    n_ext = tm + 2 * POOL_HALO
    pool_w = pc_ref.shape[1]
    grp = pool_w // len(POOL_WINDOWS)

    prev = jnp.where(i == 0, 0.0, pp_ref[...])
    nxt = jnp.where(i == last, 0.0, pn_ref[...])
    ext = jnp.concatenate([prev, pc_ref[...], nxt], axis=0)
    t = (i * tm + lax.broadcasted_iota(jnp.int32, (tm, 1), 0))

    def shift_up(x, k):
        return pltpu.roll(x, n_ext - k, axis=0)

    def shift_down(x, k):
        return pltpu.roll(x, k, axis=0)

    for gi, w in enumerate(POOL_WINDOWS):
        e = ext[:, gi * grp:(gi + 1) * grp]
        half = w // 2
        part, span = e, 1
        while span < half:
            part = part + shift_up(part, span)
            span *= 2
        win = shift_down(part, half) + part
        win = win[POOL_HALO:POOL_HALO + tm]
        lo = jnp.clip(t - half, 0, seq_len)
        hi = jnp.clip(t - half + w, 0, seq_len)
        cnt = (hi - lo).astype(F32)
        pooled = win / cnt - e[POOL_HALO:POOL_HALO + tm]
        y = jnp.dot(pooled.astype(BF16), wp_ref[gi].astype(BF16), preferred_element_type=F32)
        o_ref[:, gi * grp:(gi + 1) * grp] = (y * ps_ref[:, gi * grp:(gi + 1) * grp]).astype(o_ref.dtype)

    aw = ya_ref.shape[1]
    o_ref[:, pool_w:pool_w + aw] = ya_ref[...]

    u = cc_ref[...] * cx_ref[...]
    u_before = jnp.where(i == 0, 0.0, ccp_ref[CONV_HALO - 1:CONV_HALO, :] * cxp_ref[CONV_HALO - 1:CONV_HALO, :])
    u_after = jnp.where(i == last, 0.0, ccn_ref[0:1, :] * cxn_ref[0:1, :])
    r = lax.broadcasted_iota(jnp.int32, (tm, 1), 0)
    um1 = jnp.where(r == 0, u_before, pltpu.roll(u, 1, axis=0))
    up1 = jnp.where(r == tm - 1, u_after, pltpu.roll(u, tm - 1, axis=0))
    conv = cw_ref[0:1, :] * um1 + cw_ref[1:2, :] * u + cw_ref[2:3, :] * up1
    o_ref[:, pool_w + aw:] = (cb_ref[...] * conv).astype(o_ref.dtype)


def _mixer(p, ya, w_pool, pool_scale, conv_w, cb_col0, tm=256):
    m = p.shape[0]
    tm = min(tm, m)
    blk = pool_scale.shape[-1]
    c0 = cb_col0 // blk
    nblk_pool, nblk_conv = m // POOL_HALO, m // CONV_HALO
    rp, rc = tm // POOL_HALO, tm // CONV_HALO

    cur = lambda c: pl.BlockSpec((tm, blk), lambda i: (i, c))
    prev_p = pl.BlockSpec((POOL_HALO, blk), lambda i: (jnp.maximum(i * rp - 1, 0), 0))
    next_p = pl.BlockSpec((POOL_HALO, blk), lambda i: (jnp.minimum((i + 1) * rp, nblk_pool - 1), 0))
    prev_c = lambda c: pl.BlockSpec((CONV_HALO, blk), lambda i: (jnp.maximum(i * rc - 1, 0), c))
    next_c = lambda c: pl.BlockSpec((CONV_HALO, blk), lambda i: (jnp.minimum((i + 1) * rc, nblk_conv - 1), c))
    aw = ya.shape[1]
    full = lambda a: pl.BlockSpec(a.shape, lambda i: (0,) * a.ndim)
    ps = pool_scale.reshape(1, blk)
    return pl.pallas_call(
        functools.partial(_mixer_kernel, seq_len=m),
        grid=(m // tm,),
        in_specs=[cur(0), prev_p, next_p,
                  cur(c0), cur(c0 + 1), prev_c(c0 + 1), next_c(c0 + 1),
                  cur(c0 + 2), prev_c(c0 + 2), next_c(c0 + 2),
                  pl.BlockSpec((tm, aw), lambda i: (i, 0)),
                  full(w_pool), full(ps), full(conv_w)],
        out_specs=pl.BlockSpec((tm, 2 * blk + aw), lambda i: (i, 0)),
        out_shape=jax.ShapeDtypeStruct((m, 2 * blk + aw), BF16),
        compiler_params=_params("parallel"),
        name="mixer",
    )(p, p, p, p, p, p, p, p, p, p, ya, w_pool, ps, conv_w)


def _down(act, wd, layer, tm=1024, tn=256):
    return _proj(act, wd, layer, wd.shape[-1], tm=tm, tn=tn, a_buffers=1, out_dtype=BF16)


def _moe_tile_kernel(te_ref, nv_ref, tr_ref, a_ref, *refs, body):
    w_refs, o_ref = refs[:-1], refs[-1]
    tm = o_ref.shape[0]
    half = tm // 2
    rows = tr_ref[pl.program_id(1)]

    @pl.when(rows > half)
    def _():
        body(a_ref, *w_refs, o_ref)

    @pl.when(jnp.logical_and(rows > 0, rows <= half))
    def _():
        body(a_ref.at[pl.ds(0, half)], *w_refs, o_ref.at[pl.ds(0, half)])
        o_ref[pl.ds(half, tm - half), :] = jnp.zeros((tm - half, o_ref.shape[1]), o_ref.dtype)

    @pl.when(rows == 0)
    def _():
        o_ref[...] = jnp.zeros(o_ref.shape, o_ref.dtype)


_moe_gate_up_kernel = functools.partial(_moe_tile_kernel, body=_gate_up_kernel)
_moe_down_kernel = functools.partial(_moe_tile_kernel, body=_proj_kernel)


def _moe_call(kernel, a, weights, tile_expert, n_valid, tile_rows, n_out, out_dtype, tm, tn, name):
    r, k = a.shape
    n_tiles = r // tm
    row = lambda j, t, te, nv, tr: (jnp.minimum(t, nv[0] - 1), 0)
    wspec = pl.BlockSpec((None, k, tn), lambda j, t, te, nv, tr: (te[t], 0, j))
    return pl.pallas_call(
        kernel,
        grid_spec=pltpu.PrefetchScalarGridSpec(
            num_scalar_prefetch=3,
            grid=(n_out // tn, n_tiles),
            in_specs=[pl.BlockSpec((tm, k), row)] + [wspec] * len(weights),
            out_specs=pl.BlockSpec((tm, tn), lambda j, t, te, nv, tr: (t, j)),
        ),
        out_shape=jax.ShapeDtypeStruct((r, n_out), out_dtype),
        compiler_params=_params("parallel", "arbitrary"),
        name=name,
    )(tile_expert, n_valid, tile_rows, a, *weights)


def _moe_ffn(h, ridx, wg, wu, wd, tm=512, tn=512, tn_down=1024):
    m, d = h.shape
    n_slots = m * TOP_K
    r_max = n_slots + N_EXPERTS * tm
    n_tiles = r_max // tm
    e_flat = ridx[:, :TOP_K].reshape(n_slots)
    order = jnp.argsort(e_flat, stable=True).astype(jnp.int32)
    counts = jnp.sum((e_flat[:, None] == jnp.arange(N_EXPERTS)[None, :]).astype(jnp.int32), axis=0)
    padded = ((counts + tm - 1) // tm) * tm
    gend = jnp.cumsum(padded)
    gstart = gend - padded
    cstart = jnp.cumsum(counts) - counts
    n_valid = (gend[-1] // tm).astype(jnp.int32).reshape(1)
    tile_expert = jnp.sum((gend[None, :] <= (jnp.arange(n_tiles) * tm)[:, None]).astype(jnp.int32), axis=1)
    tile_expert = jnp.minimum(tile_expert, N_EXPERTS - 1)
    last_e = tile_expert[jnp.maximum(n_valid[0] - 1, 0)]
    tile_expert = jnp.where(jnp.arange(n_tiles) < n_valid[0], tile_expert, last_e)
    rows = jnp.arange(r_max)
    row_e = tile_expert[rows // tm]
    off = rows - gstart[row_e]
    row_valid = (off < counts[row_e]) & (rows < gend[-1])
    src = jnp.clip(cstart[row_e] + off, 0, n_slots - 1)
    row_token = jnp.where(row_valid, order[src] // TOP_K, rows % m)
    tile_lo = jnp.arange(n_tiles) * tm
    tile_rows = jnp.clip(gstart[tile_expert] + counts[tile_expert] - tile_lo, 0, tm)
    tile_rows = jnp.where(jnp.arange(n_tiles) < n_valid[0], tile_rows, 0).astype(jnp.int32)
    rank = jnp.argsort(order).astype(jnp.int32)
    slot_row = (gstart[e_flat] + rank - cstart[e_flat]).reshape(m, TOP_K)

    hs = h.at[row_token].get(mode="promise_in_bounds")
    act = _moe_call(_moe_gate_up_kernel, hs, (wg, wu), tile_expert, n_valid, tile_rows, wg.shape[-1], BF16,
                    tm, tn, "moe_gate_up")
    ys = _moe_call(_moe_down_kernel, act, (wd,), tile_expert, n_valid, tile_rows, d, BF16, tm, tn_down,
                   "moe_down")
    return ys, slot_row


def _combine_kernel(x_ref, y_ref, gw_ref, gate_ref, g_ref, o_ref):
    gw = gw_ref[...]
    d = x_ref.shape[1]
    y = gw[:, 0:1] * y_ref[:, 0:d].astype(F32) + gw[:, 1:2] * y_ref[:, d:].astype(F32)
    o_ref[...] = _rms(x_ref[...] + gate_ref[...] * y, g_ref[...])


def _combine_final(x, y12, gw, gate, g, tm=256):
    m, d = x.shape
    row = pl.BlockSpec((tm, d), lambda i: (i, 0))
    vec = pl.BlockSpec((1, d), lambda i: (0, 0))
    return pl.pallas_call(
        _combine_kernel,
        grid=(m // tm,),
        in_specs=[row, pl.BlockSpec((tm, TOP_K * d), lambda i: (i, 0)),
                  pl.BlockSpec((tm, LANES), lambda i: (i, 0)), vec, vec],
        out_specs=row,
        out_shape=jax.ShapeDtypeStruct((m, d), F32),
        compiler_params=_params("parallel"),
        name="combine_final",
    )(x, y12, gw, gate, g)


def _rope_tables(seq_len):
    rows = seq_len // GRID_W
    row = jnp.repeat(jnp.arange(rows), GRID_W).astype(F32)
    col = jnp.tile(jnp.arange(GRID_W), rows).astype(F32)
    n_axis = HEAD_DIM // 4
    inv = ROPE_BASE ** (-jnp.arange(n_axis, dtype=F32) / n_axis)
    ang = jnp.concatenate([row[:, None] * inv, col[:, None] * inv], axis=-1)
    cos, sin = jnp.cos(ang), jnp.sin(ang)
    return jnp.concatenate([cos, cos], axis=-1), jnp.concatenate([-sin, sin], axis=-1)


def kernel(x, c, ctx, c_ctx, w_mod, b_mod, g_mix, w_in, w_pool, pool_scale, g_q, g_k, conv_w, w_out, g_ffn,
           w_gate_dense, w_up_dense, w_down_dense, w_router, b_router, w_gate_exp, w_up_exp, w_down_exp, g_final):
    batch, seq, d = x.shape
    depth = w_mod.shape[0]
    assert batch == 1 and depth == 2
    n_ctx = ctx.shape[1]
    pool_w = pool_scale.shape[-1]
    conv_wd = conv_w.shape[-1]
    q_w = (d // (2 * HEAD_DIM)) * HEAD_DIM
    kv_w = q_w // KV_GROUP
    q_off = pool_w
    k_off = q_off + q_w
    cb_off = k_off + 2 * kv_w
    in_w = cb_off + 3 * conv_wd

    xs, cs = x[0], ctx[0]
    cvec = jnp.zeros((SUBLANES, d), F32).at[0].set(c[0]).at[1].set(c_ctx)
    mods = _mods(cvec, w_mod, b_mod)

    def mod(l, row, k):
        return mods[l, row:row + 1, k * d:(k + 1) * d]

    cosf, sinf = _rope_tables(seq)
    cos_c, sin_c = jnp.ones((n_ctx, HEAD_DIM), F32), jnp.zeros((n_ctx, HEAD_DIM), F32)
    qscale = HEAD_DIM ** -0.5 * LOG2E

    pend_x = pend_c = None
    for l in range(depth):
        last = l == depth - 1
        vec = lambda a: a[l].reshape(1, -1)
        gq, gk = vec(g_q) * qscale, vec(g_k)

        def first_norm(stream, pend, row):
            if pend is None:
                (h,) = _norm(stream, vec(g_mix), mod(l, row, 0), mod(l, row, 1))
                return stream, h
            return _norm(stream, vec(g_mix), mod(l, row, 0), mod(l, row, 1), resid=pend)

        cs, hc = first_norm(cs, pend_c, 1)
        if last:
            pkv = _proj(hc, w_in, l, 2 * kv_w, col0=k_off)
            kc, vtc = _qkv_prep(pkv, cos_c, sin_c, gq, gk, None, 0)
        else:
            pc = _proj(hc, w_in, l, in_w)
            qtc, kc, vtc = _qkv_prep(pc, cos_c, sin_c, gq, gk, q_off, k_off)
            yac = _flash_t(qtc, kc, vtc)
            mixc = _mixer(pc, yac, w_pool[l], pool_scale[l], conv_w[l], cb_off)
            ymc = _proj(mixc, w_out, l, d, out_dtype=BF16)

        xs, hx = first_norm(xs, pend_x, 0)
        px = _proj(hx, w_in, l, in_w)
        qtx, kx, vtx = _qkv_prep(px, cosf, sinf, gq, gk, q_off, k_off)
        k_all = jnp.concatenate([kc, kx], axis=0)
        vt_all = jnp.concatenate([vtc, vtx], axis=1)
        yax = _flash_t(qtx, k_all, vt_all)
        mixx = _mixer(px, yax, w_pool[l], pool_scale[l], conv_w[l], cb_off)
        ymx = _proj(mixx, w_out, l, d, out_dtype=BF16)

        i = l // 2
        if l % 2 == 0:
            def ffn(stream, ym, row):
                xn, h = _norm(stream, vec(g_ffn), mod(l, row, 3), mod(l, row, 4), resid=(ym, mod(l, row, 2)))
                act = _gate_up(h, w_gate_dense, w_up_dense, i)
                return xn, _down(act, w_down_dense, i)

            xs, yfx = ffn(xs, ymx, 0)
            pend_x = (yfx, mod(l, 0, 5))
            if not last:
                cs, yfc = ffn(cs, ymc, 1)
                pend_c = (yfc, mod(l, 1, 5))
        else:
            wr = jnp.zeros((d, LANES), F32).at[:, :N_EXPERTS].set(w_router[i])
            br = jnp.full((1, LANES), NEG_BIG, F32).at[0, :N_EXPERTS].set(b_router[i])
            xs, hf, ridx, rgw = _norm(xs, vec(g_ffn), mod(l, 0, 3), mod(l, 0, 4), resid=(ymx, mod(l, 0, 2)),
                                      router=(wr, br), mode="route")
            ys, slot_row = _moe_ffn(hf, ridx, w_gate_exp[i], w_up_exp[i], w_down_exp[i])
            y12 = ys.at[slot_row.reshape(-1)].get(mode="promise_in_bounds").reshape(seq, TOP_K * d)
            pend_x = (y12, rgw, mod(l, 0, 5))

    if len(pend_x) == 3:
        out = _combine_final(xs, pend_x[0], pend_x[1], pend_x[2], g_final.reshape(1, -1))
    else:
        (out,) = _norm(xs, g_final.reshape(1, -1), resid=pend_x, mode="final")
    return out[None]
```

```python
import functools

import jax
import jax.numpy as jnp
from jax import lax
from jax.experimental import pallas as pl
from jax.experimental.pallas import tpu as pltpu

F32 = jnp.float32
BF16 = jnp.bfloat16

POOL_WINDOWS = (2, 4, 8, 16)
HEAD_DIM = 128
KV_GROUP = 4
N_MOD = 6
N_EXPERTS = 8
TOP_K = 2
GRID_W = 64
ROPE_BASE = 10000.0
EPS = 1e-6

LANES = 128
SUBLANES = 8
VMEM_LIMIT_BYTES = 56 * 1024 * 1024

POOL_HALO = 16
CONV_HALO = 8
NEG_BIG = -1e30
LOG2E = 1.4426950408889634


def _params(*sem):
    return pltpu.CompilerParams(dimension_semantics=sem, vmem_limit_bytes=VMEM_LIMIT_BYTES)


def _silu(x):
    return x * jax.nn.sigmoid(x)


def _mods_kernel(c_ref, w_ref, b_ref, o_ref):
    s = _silu(c_ref[...]).astype(BF16)
    w = w_ref[...].astype(BF16)
    o_ref[...] = jnp.dot(s, w, preferred_element_type=F32) + b_ref[...]


def _mods(cvec, w_mod, b_mod, tn=512):
    depth, d, n = w_mod.shape
    return pl.pallas_call(
        _mods_kernel,
        grid=(depth, n // tn),
        in_specs=[
            pl.BlockSpec((SUBLANES, d), lambda l, j: (0, 0)),
            pl.BlockSpec((None, d, tn), lambda l, j: (l, 0, j)),
            pl.BlockSpec((None, 1, tn), lambda l, j: (l, 0, j)),
        ],
        out_specs=pl.BlockSpec((None, SUBLANES, tn), lambda l, j: (l, 0, j)),
        out_shape=jax.ShapeDtypeStruct((depth, SUBLANES, n), F32),
        compiler_params=_params("parallel", "parallel"),
        name="mods",
    )(cvec, w_mod, b_mod.reshape(depth, 1, n))


def _rms(x, g):
    return (x * lax.rsqrt(jnp.mean(x * x, axis=-1, keepdims=True) + EPS)) * g


def _norm_kernel(*refs, has_resid, mode):
    it = iter(refs)
    x_ref = next(it)
    x = x_ref[...]
    if has_resid:
        y_ref, gate_ref = next(it), next(it)
        x = x + gate_ref[...] * y_ref[...].astype(F32)
    g_ref = next(it)
    if mode == "final":
        o_ref = next(it)
        o_ref[...] = _rms(x, g_ref[...])
        return
    shift_ref, scale_ref = next(it), next(it)
    if mode == "route":
        wr_ref, br_ref = next(it), next(it)
    if has_resid:
        xo_ref = next(it)
        xo_ref[...] = x
    h_ref = next(it)
    h = _rms(x, g_ref[...]) * (1.0 + scale_ref[...]) + shift_ref[...]
    h_ref[...] = h.astype(BF16)
    if mode == "route":
        idx_ref, gw_ref = next(it), next(it)
        logits = jnp.dot(h, wr_ref[...], preferred_element_type=F32,
                         precision=lax.Precision.HIGHEST) + br_ref[...]
        lane = lax.broadcasted_iota(jnp.int32, logits.shape, 1).astype(F32)
        m1 = jnp.max(logits, axis=-1, keepdims=True)
        i1 = jnp.min(jnp.where(logits == m1, lane, float(LANES)), axis=-1, keepdims=True)
        rest = jnp.where(lane == i1, -jnp.inf, logits)
        m2 = jnp.max(rest, axis=-1, keepdims=True)
        i2 = jnp.min(jnp.where(rest == m2, lane, float(LANES)), axis=-1, keepdims=True)
        e = jnp.exp(m2 - m1)
        den = 1.0 + e
        idx_ref[...] = jnp.where(lane == 0.0, i1, jnp.where(lane == 1.0, i2, 0.0)).astype(jnp.int32)
        gw_ref[...] = jnp.where(lane == 0.0, 1.0 / den, jnp.where(lane == 1.0, e / den, 0.0))


def _norm(x, g, shift=None, scale=None, resid=None, router=None, mode="mod", tm=256):
    m, d = x.shape
    tm = min(tm, m)
    row = pl.BlockSpec((tm, d), lambda i: (i, 0))
    vec = pl.BlockSpec((1, d), lambda i: (0, 0))
    lane_blk = pl.BlockSpec((tm, LANES), lambda i: (i, 0))
    args, in_specs = [x], [row]
    if resid is not None:
        args += [resid[0], resid[1]]
        in_specs += [row, vec]
    args.append(g)
    in_specs.append(vec)
    out_shape, out_specs = [], []
    if mode != "final":
        args += [shift, scale]
        in_specs += [vec, vec]
        if mode == "route":
            args += [router[0], router[1]]
            in_specs += [pl.BlockSpec((d, LANES), lambda i: (0, 0)), pl.BlockSpec((1, LANES), lambda i: (0, 0))]
        if resid is not None:
            out_shape.append(jax.ShapeDtypeStruct((m, d), F32))
            out_specs.append(row)
        out_shape.append(jax.ShapeDtypeStruct((m, d), BF16))
        out_specs.append(row)
        if mode == "route":
            out_shape += [jax.ShapeDtypeStruct((m, LANES), jnp.int32), jax.ShapeDtypeStruct((m, LANES), F32)]
            out_specs += [lane_blk, lane_blk]
    else:
        out_shape.append(jax.ShapeDtypeStruct((m, d), F32))
        out_specs.append(row)
    return pl.pallas_call(
        functools.partial(_norm_kernel, has_resid=resid is not None, mode=mode),
        grid=(m // tm,),
        in_specs=in_specs,
        out_specs=out_specs,
        out_shape=out_shape,
        compiler_params=_params("parallel"),
        name="norm_" + mode,
    )(*args)


def _proj_kernel(a_ref, w_ref, o_ref):
    o_ref[...] = jnp.dot(a_ref[...], w_ref[...].astype(BF16), preferred_element_type=F32).astype(o_ref.dtype)


def _proj(a, w, layer, n_out, col0=0, tm=1024, tn=512, out_dtype=F32, a_buffers=2):
    m, k = a.shape
    tm = min(tm, m)
    cb = col0 // tn
    a_mode = {} if a_buffers == 2 else {"pipeline_mode": pl.Buffered(a_buffers)}
    return pl.pallas_call(
        _proj_kernel,
        grid=(m // tm, n_out // tn),
        in_specs=[
            pl.BlockSpec((tm, k), lambda i, j: (i, 0), **a_mode),
            pl.BlockSpec((None, k, tn), lambda i, j: (layer, 0, j + cb)),
        ],
        out_specs=pl.BlockSpec((tm, tn), lambda i, j: (i, j)),
        out_shape=jax.ShapeDtypeStruct((m, n_out), out_dtype),
        compiler_params=_params("parallel", "arbitrary"),
        name="proj",
    )(a, w)


def _gate_up_kernel(a_ref, wg_ref, wu_ref, o_ref):
    a = a_ref[...]
    g = jnp.dot(a, wg_ref[...].astype(BF16), preferred_element_type=F32)
    u = jnp.dot(a, wu_ref[...].astype(BF16), preferred_element_type=F32)
    o_ref[...] = (_silu(g) * u).astype(o_ref.dtype)


def _gate_up(a, wg, wu, layer, tm=1024, tn=256):
    m, k = a.shape
    n = wg.shape[-1]
    tm = min(tm, m)
    wspec = pl.BlockSpec((None, k, tn), lambda i, j: (layer, 0, j))
    return pl.pallas_call(
        _gate_up_kernel,
        grid=(m // tm, n // tn),
        in_specs=[pl.BlockSpec((tm, k), lambda i, j: (i, 0)), wspec, wspec],
        out_specs=pl.BlockSpec((tm, tn), lambda i, j: (i, j)),
        out_shape=jax.ShapeDtypeStruct((m, n), BF16),
        compiler_params=_params("parallel", "arbitrary"),
        name="gate_up",
    )(a, wg, wu)


def _qkv_kernel(*refs, want_q):
    if want_q:
        q0_ref, q1_ref, kv_ref, cos_ref, sin_ref, gq_ref, gk_ref, qt_ref, k_ref, vt_ref = refs
    else:
        kv_ref, cos_ref, sin_ref, gk_ref, k_ref, vt_ref = refs
    cos, sin = cos_ref[...], sin_ref[...]
    tm = cos.shape[0]

    def head(x, g):
        y = _rms(x, g)
        return y * cos + pltpu.roll(y, HEAD_DIM // 2, axis=1) * sin

    if want_q:
        heads_per_blk = q0_ref.shape[1] // HEAD_DIM
        for b, src in enumerate((q0_ref, q1_ref)):
            for hh in range(heads_per_blk):
                lo = hh * HEAD_DIM
                kv, g = divmod(b * heads_per_blk + hh, KV_GROUP)
                qh = head(src[:, lo:lo + HEAD_DIM], gq_ref[...])
                qt_ref[kv, :, g * tm:(g + 1) * tm] = qh.T.astype(BF16)
    kvw = k_ref.shape[1]
    for hh in range(kvw // HEAD_DIM):
        lo = hh * HEAD_DIM
        k_ref[:, lo:lo + HEAD_DIM] = head(kv_ref[:, lo:lo + HEAD_DIM], gk_ref[...]).astype(BF16)
    vt_ref[...] = kv_ref[:, kvw:].T.astype(BF16)


def _qkv_prep(p, cosf, sinf, gq_scaled, gk, q_col0, kv_col0, tm=512):
    m = p.shape[0]
    tm = min(tm, m)
    blk = 2 * KV_GROUP * HEAD_DIM
    kvw = blk // 2
    hkv = kvw // HEAD_DIM
    want_q = q_col0 is not None
    pspec = lambda c: pl.BlockSpec((tm, blk), lambda i: (i, c))
    tab = pl.BlockSpec((tm, HEAD_DIM), lambda i: (i, 0))
    gvec = pl.BlockSpec((1, HEAD_DIM), lambda i: (0, 0))
    kv_specs = [pl.BlockSpec((tm, kvw), lambda i: (i, 0)), pl.BlockSpec((kvw, tm), lambda i: (0, i))]
    kv_shapes = [jax.ShapeDtypeStruct((m, kvw), BF16), jax.ShapeDtypeStruct((kvw, m), BF16)]
    if want_q:
        args = (p, p, p, cosf, sinf, gq_scaled, gk)
        in_specs = [pspec(q_col0 // blk), pspec(q_col0 // blk + 1), pspec(kv_col0 // blk), tab, tab, gvec, gvec]
        out_specs = [pl.BlockSpec((hkv, None, HEAD_DIM, KV_GROUP * tm), lambda i: (0, i, 0, 0))] + kv_specs
        out_shape = [jax.ShapeDtypeStruct((hkv, m // tm, HEAD_DIM, KV_GROUP * tm), BF16)] + kv_shapes
    else:
        args = (p, cosf, sinf, gk)
        in_specs = [pspec(kv_col0 // blk), tab, tab, gvec]
        out_specs = kv_specs
        out_shape = kv_shapes
    return pl.pallas_call(
        functools.partial(_qkv_kernel, want_q=want_q),
        grid=(m // tm,),
        in_specs=in_specs,
        out_specs=out_specs,
        out_shape=out_shape,
        compiler_params=_params("parallel"),
        name="qkv_prep",
    )(*args)


def _flash_t_kernel(qn_ref, kn_ref, qc_ref, kc_ref, vt_ref, o_ref,
                    s_a, s_b, mx_a, mx_b, m_ref, l_ref, acc_ref):
    h, i, j = pl.program_id(0), pl.program_id(1), pl.program_id(2)
    nq, nk = pl.num_programs(1), pl.num_programs(2)
    step = (h * nq + i) * nk + j

    def scores(k_ref, q_ref, s_ref, mx_ref):
        s = jnp.dot(k_ref[...], q_ref[...], preferred_element_type=F32)
        s_ref[...] = s
        mx_ref[...] = jnp.max(s, axis=0, keepdims=True)

    @pl.when(step == 0)
    def _():
        scores(kc_ref, qc_ref, s_a, mx_a)

    @pl.when(j == 0)
    def _():
        m_ref[...] = jnp.full(m_ref.shape, -jnp.inf, F32)
        l_ref[...] = jnp.zeros(l_ref.shape, F32)
        acc_ref[...] = jnp.zeros(acc_ref.shape, F32)

    def body(s_cur, mx_cur, s_nxt, mx_nxt):
        scores(kn_ref, qn_ref, s_nxt, mx_nxt)
        m_prev = m_ref[...]
        m_new = jnp.maximum(m_prev, mx_cur[...])
        alpha = jnp.exp2(m_prev - m_new)
        p = jnp.exp2(s_cur[...] - m_new)
        l_ref[...] = alpha * l_ref[...] + jnp.sum(p, axis=0, keepdims=True)
        acc_ref[...] = alpha * acc_ref[...] + jnp.dot(vt_ref[...], p.astype(BF16), preferred_element_type=F32)
        m_ref[...] = m_new

    @pl.when(step % 2 == 0)
    def _():
        body(s_a, mx_a, s_b, mx_b)

    @pl.when(step % 2 == 1)
    def _():
        body(s_b, mx_b, s_a, mx_a)

    @pl.when(j == nk - 1)
    def _():
        tq = o_ref.shape[0]
        o = acc_ref[...] / l_ref[...]
        for g in range(KV_GROUP):
            o_ref[:, g * HEAD_DIM:(g + 1) * HEAD_DIM] = o[:, g * tq:(g + 1) * tq].T.astype(o_ref.dtype)


def _flash_t(qt, k, vt, tk=1408):
    hkv, nq, _, nqc = qt.shape
    tq = nqc // KV_GROUP
    lk = k.shape[0]
    tk = min(tk, lk)
    nk = lk // tk
    gw = KV_GROUP * HEAD_DIM

    def nxt(h, i, j):
        j1 = j + 1
        i1 = i + j1 // nk
        h1 = h + i1 // nq
        return jnp.minimum(h1, hkv - 1), i1 % nq, j1 % nk

    def qn_map(h, i, j):
        h1, i1, _ = nxt(h, i, j)
        return h1, i1, 0, 0

    def kn_map(h, i, j):
        h1, _, j1 = nxt(h, i, j)
        return j1, h1

    qspec = lambda m: pl.BlockSpec((None, None, HEAD_DIM, nqc), m)
    return pl.pallas_call(
        _flash_t_kernel,
        grid=(hkv, nq, nk),
        in_specs=[
            qspec(qn_map),
            pl.BlockSpec((tk, HEAD_DIM), kn_map),
            qspec(lambda h, i, j: (h, i, 0, 0)),
            pl.BlockSpec((tk, HEAD_DIM), lambda h, i, j: (0, h)),
            pl.BlockSpec((HEAD_DIM, tk), lambda h, i, j: (h, j)),
        ],
        out_specs=pl.BlockSpec((tq, gw), lambda h, i, j: (i, h)),
        out_shape=jax.ShapeDtypeStruct((nq * tq, hkv * gw), BF16),
        scratch_shapes=[pltpu.VMEM((tk, nqc), F32), pltpu.VMEM((tk, nqc), F32),
                        pltpu.VMEM((1, nqc), F32), pltpu.VMEM((1, nqc), F32),
                        pltpu.VMEM((1, nqc), F32), pltpu.VMEM((1, nqc), F32),
                        pltpu.VMEM((HEAD_DIM, nqc), F32)],
        compiler_params=_params("arbitrary", "arbitrary", "arbitrary"),
        name="flash_t",
    )(qt, k, qt, k, vt)


def _mixer_kernel(pc_ref, pp_ref, pn_ref, cb_ref, cc_ref, ccp_ref, ccn_ref, cx_ref, cxp_ref, cxn_ref,
                  ya_ref, wp_ref, ps_ref, cw_ref, o_ref, *, seq_len):
    i = pl.program_id(0)
    last = pl.num_programs(0) - 1
    tm = pc_ref.shape[0]
    n_ext = tm + 2 * POOL_HALO
    pool_w = pc_ref.shape[1]
    grp = pool_w // len(POOL_WINDOWS)

    prev = jnp.where(i == 0, 0.0, pp_ref[...])
    nxt = jnp.where(i == last, 0.0, pn_ref[...])
    ext = jnp.concatenate([prev, pc_ref[...], nxt], axis=0)
    t = (i * tm + lax.broadcasted_iota(jnp.int32, (tm, 1), 0))

    def shift_up(x, k):
        return pltpu.roll(x, n_ext - k, axis=0)

    def shift_down(x, k):
        return pltpu.roll(x, k, axis=0)

    for gi, w in enumerate(POOL_WINDOWS):
        e = ext[:, gi * grp:(gi + 1) * grp]
        half = w // 2
        part, span = e, 1
        while span < half:
            part = part + shift_up(part, span)
            span *= 2
        win = shift_down(part, half) + part
        win = win[POOL_HALO:POOL_HALO + tm]
        lo = jnp.clip(t - half, 0, seq_len)
        hi = jnp.clip(t - half + w, 0, seq_len)
        cnt = (hi - lo).astype(F32)
        pooled = win / cnt - e[POOL_HALO:POOL_HALO + tm]
        y = jnp.dot(pooled.astype(BF16), wp_ref[gi].astype(BF16), preferred_element_type=F32)
        o_ref[:, gi * grp:(gi + 1) * grp] = (y * ps_ref[:, gi * grp:(gi + 1) * grp]).astype(o_ref.dtype)

    aw = ya_ref.shape[1]
    o_ref[:, pool_w:pool_w + aw] = ya_ref[...]

    u = cc_ref[...] * cx_ref[...]
    u_before = jnp.where(i == 0, 0.0, ccp_ref[CONV_HALO - 1:CONV_HALO, :] * cxp_ref[CONV_HALO - 1:CONV_HALO, :])
    u_after = jnp.where(i == last, 0.0, ccn_ref[0:1, :] * cxn_ref[0:1, :])
    r = lax.broadcasted_iota(jnp.int32, (tm, 1), 0)
    um1 = jnp.where(r == 0, u_before, pltpu.roll(u, 1, axis=0))
    up1 = jnp.where(r == tm - 1, u_after, pltpu.roll(u, tm - 1, axis=0))
    conv = cw_ref[0:1, :] * um1 + cw_ref[1:2, :] * u + cw_ref[2:3, :] * up1
    o_ref[:, pool_w + aw:] = (cb_ref[...] * conv).astype(o_ref.dtype)


def _mixer(p, ya, w_pool, pool_scale, conv_w, cb_col0, tm=256):
    m = p.shape[0]
    tm = min(tm, m)
    blk = pool_scale.shape[-1]
    c0 = cb_col0 // blk
    nblk_pool, nblk_conv = m // POOL_HALO, m // CONV_HALO
    rp, rc = tm // POOL_HALO, tm // CONV_HALO

    cur = lambda c: pl.BlockSpec((tm, blk), lambda i: (i, c))
    prev_p = pl.BlockSpec((POOL_HALO, blk), lambda i: (jnp.maximum(i * rp - 1, 0), 0))
    next_p = pl.BlockSpec((POOL_HALO, blk), lambda i: (jnp.minimum((i + 1) * rp, nblk_pool - 1), 0))
    prev_c = lambda c: pl.BlockSpec((CONV_HALO, blk), lambda i: (jnp.maximum(i * rc - 1, 0), c))
    next_c = lambda c: pl.BlockSpec((CONV_HALO, blk), lambda i: (jnp.minimum((i + 1) * rc, nblk_conv - 1), c))
    aw = ya.shape[1]
    full = lambda a: pl.BlockSpec(a.shape, lambda i: (0,) * a.ndim)
    ps = pool_scale.reshape(1, blk)
    return pl.pallas_call(
        functools.partial(_mixer_kernel, seq_len=m),
        grid=(m // tm,),
        in_specs=[cur(0), prev_p, next_p,
                  cur(c0), cur(c0 + 1), prev_c(c0 + 1), next_c(c0 + 1),
                  cur(c0 + 2), prev_c(c0 + 2), next_c(c0 + 2),
                  pl.BlockSpec((tm, aw), lambda i: (i, 0)),
                  full(w_pool), full(ps), full(conv_w)],
        out_specs=pl.BlockSpec((tm, 2 * blk + aw), lambda i: (i, 0)),
        out_shape=jax.ShapeDtypeStruct((m, 2 * blk + aw), BF16),
        compiler_params=_params("parallel"),
        name="mixer",
    )(p, p, p, p, p, p, p, p, p, p, ya, w_pool, ps, conv_w)


def _down(act, wd, layer, tm=1024, tn=256):
    return _proj(act, wd, layer, wd.shape[-1], tm=tm, tn=tn, a_buffers=1, out_dtype=BF16)


def _moe_tile_kernel(te_ref, nv_ref, tr_ref, a_ref, *refs, body):
    w_refs, o_ref = refs[:-1], refs[-1]
    tm = o_ref.shape[0]
    half = tm // 2
    rows = tr_ref[pl.program_id(1)]

    @pl.when(rows > half)
    def _():
        body(a_ref, *w_refs, o_ref)

    @pl.when(jnp.logical_and(rows > 0, rows <= half))
    def _():
        body(a_ref.at[pl.ds(0, half)], *w_refs, o_ref.at[pl.ds(0, half)])
        o_ref[pl.ds(half, tm - half), :] = jnp.zeros((tm - half, o_ref.shape[1]), o_ref.dtype)

    @pl.when(rows == 0)
    def _():
        o_ref[...] = jnp.zeros(o_ref.shape, o_ref.dtype)


_moe_gate_up_kernel = functools.partial(_moe_tile_kernel, body=_gate_up_kernel)
_moe_down_kernel = functools.partial(_moe_tile_kernel, body=_proj_kernel)


def _moe_call(kernel, a, weights, tile_expert, n_valid, tile_rows, n_out, out_dtype, tm, tn, name):
    r, k = a.shape
    n_tiles = r // tm
    row = lambda j, t, te, nv, tr: (jnp.minimum(t, nv[0] - 1), 0)
    wspec = pl.BlockSpec((None, k, tn), lambda j, t, te, nv, tr: (te[t], 0, j))
    return pl.pallas_call(
        kernel,
        grid_spec=pltpu.PrefetchScalarGridSpec(
            num_scalar_prefetch=3,
            grid=(n_out // tn, n_tiles),
            in_specs=[pl.BlockSpec((tm, k), row)] + [wspec] * len(weights),
            out_specs=pl.BlockSpec((tm, tn), lambda j, t, te, nv, tr: (t, j)),
        ),
        out_shape=jax.ShapeDtypeStruct((r, n_out), out_dtype),
        compiler_params=_params("parallel", "arbitrary"),
        name=name,
    )(tile_expert, n_valid, tile_rows, a, *weights)


def _moe_ffn(h, ridx, wg, wu, wd, tm=512, tn=512, tn_down=1024):
    m, d = h.shape
    n_slots = m * TOP_K
    r_max = n_slots + N_EXPERTS * tm
    n_tiles = r_max // tm
    e_flat = ridx[:, :TOP_K].reshape(n_slots)
    order = jnp.argsort(e_flat, stable=True).astype(jnp.int32)
    counts = jnp.sum((e_flat[:, None] == jnp.arange(N_EXPERTS)[None, :]).astype(jnp.int32), axis=0)
    padded = ((counts + tm - 1) // tm) * tm
    gend = jnp.cumsum(padded)
    gstart = gend - padded
    cstart = jnp.cumsum(counts) - counts
    n_valid = (gend[-1] // tm).astype(jnp.int32).reshape(1)
    tile_expert = jnp.sum((gend[None, :] <= (jnp.arange(n_tiles) * tm)[:, None]).astype(jnp.int32), axis=1)
    tile_expert = jnp.minimum(tile_expert, N_EXPERTS - 1)
    last_e = tile_expert[jnp.maximum(n_valid[0] - 1, 0)]
    tile_expert = jnp.where(jnp.arange(n_tiles) < n_valid[0], tile_expert, last_e)
    rows = jnp.arange(r_max)
    row_e = tile_expert[rows // tm]
    off = rows - gstart[row_e]
    row_valid = (off < counts[row_e]) & (rows < gend[-1])
    src = jnp.clip(cstart[row_e] + off, 0, n_slots - 1)
    row_token = jnp.where(row_valid, order[src] // TOP_K, rows % m)
    tile_lo = jnp.arange(n_tiles) * tm
    tile_rows = jnp.clip(gstart[tile_expert] + counts[tile_expert] - tile_lo, 0, tm)
    tile_rows = jnp.where(jnp.arange(n_tiles) < n_valid[0], tile_rows, 0).astype(jnp.int32)
    rank = jnp.argsort(order).astype(jnp.int32)
    slot_row = (gstart[e_flat] + rank - cstart[e_flat]).reshape(m, TOP_K)

    hs = h.at[row_token].get(mode="promise_in_bounds")
    act = _moe_call(_moe_gate_up_kernel, hs, (wg, wu), tile_expert, n_valid, tile_rows, wg.shape[-1], BF16,
                    tm, tn, "moe_gate_up")
    ys = _moe_call(_moe_down_kernel, act, (wd,), tile_expert, n_valid, tile_rows, d, BF16, tm, tn_down,
                   "moe_down")
    return ys, slot_row


def _combine_kernel(x_ref, y1_ref, y2_ref, gw_ref, gate_ref, g_ref, o_ref):
    gw = gw_ref[...]
    y = gw[:, 0:1] * y1_ref[...].astype(F32) + gw[:, 1:2] * y2_ref[...].astype(F32)
    o_ref[...] = _rms(x_ref[...] + gate_ref[...] * y, g_ref[...])


def _combine_final(x, y1, y2, gw, gate, g, tm=256):
    m, d = x.shape
    row = pl.BlockSpec((tm, d), lambda i: (i, 0))
    vec = pl.BlockSpec((1, d), lambda i: (0, 0))
    return pl.pallas_call(
        _combine_kernel,
        grid=(m // tm,),
        in_specs=[row, row, row, pl.BlockSpec((tm, LANES), lambda i: (i, 0)), vec, vec],
        out_specs=row,
        out_shape=jax.ShapeDtypeStruct((m, d), F32),
        compiler_params=_params("parallel"),
        name="combine_final",
    )(x, y1, y2, gw, gate, g)


def _rope_tables(seq_len):
    rows = seq_len // GRID_W
    row = jnp.repeat(jnp.arange(rows), GRID_W).astype(F32)
    col = jnp.tile(jnp.arange(GRID_W), rows).astype(F32)
    n_axis = HEAD_DIM // 4
    inv = ROPE_BASE ** (-jnp.arange(n_axis, dtype=F32) / n_axis)
    ang = jnp.concatenate([row[:, None] * inv, col[:, None] * inv], axis=-1)
    cos, sin = jnp.cos(ang), jnp.sin(ang)
    return jnp.concatenate([cos, cos], axis=-1), jnp.concatenate([-sin, sin], axis=-1)


def kernel(x, c, ctx, c_ctx, w_mod, b_mod, g_mix, w_in, w_pool, pool_scale, g_q, g_k, conv_w, w_out, g_ffn,
           w_gate_dense, w_up_dense, w_down_dense, w_router, b_router, w_gate_exp, w_up_exp, w_down_exp, g_final):
    batch, seq, d = x.shape
    depth = w_mod.shape[0]
    assert batch == 1 and depth == 2
    n_ctx = ctx.shape[1]
    pool_w = pool_scale.shape[-1]
    conv_wd = conv_w.shape[-1]
    q_w = (d // (2 * HEAD_DIM)) * HEAD_DIM
    kv_w = q_w // KV_GROUP
    q_off = pool_w
    k_off = q_off + q_w
    cb_off = k_off + 2 * kv_w
    in_w = cb_off + 3 * conv_wd

    xs, cs = x[0], ctx[0]
    cvec = jnp.zeros((SUBLANES, d), F32).at[0].set(c[0]).at[1].set(c_ctx)
    mods = _mods(cvec, w_mod, b_mod)

    def mod(l, row, k):
        return mods[l, row:row + 1, k * d:(k + 1) * d]

    cosf, sinf = _rope_tables(seq)
    cos_c, sin_c = jnp.ones((n_ctx, HEAD_DIM), F32), jnp.zeros((n_ctx, HEAD_DIM), F32)
    qscale = HEAD_DIM ** -0.5 * LOG2E

    pend_x = pend_c = None
    for l in range(depth):
        last = l == depth - 1
        vec = lambda a: a[l].reshape(1, -1)
        gq, gk = vec(g_q) * qscale, vec(g_k)

        def first_norm(stream, pend, row):
            if pend is None:
                (h,) = _norm(stream, vec(g_mix), mod(l, row, 0), mod(l, row, 1))
                return stream, h
            return _norm(stream, vec(g_mix), mod(l, row, 0), mod(l, row, 1), resid=pend)

        cs, hc = first_norm(cs, pend_c, 1)
        if last:
            pkv = _proj(hc, w_in, l, 2 * kv_w, col0=k_off)
            kc, vtc = _qkv_prep(pkv, cos_c, sin_c, gq, gk, None, 0)
        else:
            pc = _proj(hc, w_in, l, in_w)
            qtc, kc, vtc = _qkv_prep(pc, cos_c, sin_c, gq, gk, q_off, k_off)
            yac = _flash_t(qtc, kc, vtc)
            mixc = _mixer(pc, yac, w_pool[l], pool_scale[l], conv_w[l], cb_off)
            ymc = _proj(mixc, w_out, l, d, out_dtype=BF16)

        xs, hx = first_norm(xs, pend_x, 0)
        px = _proj(hx, w_in, l, in_w)
        qtx, kx, vtx = _qkv_prep(px, cosf, sinf, gq, gk, q_off, k_off)
        k_all = jnp.concatenate([kc, kx], axis=0)
        vt_all = jnp.concatenate([vtc, vtx], axis=1)
        yax = _flash_t(qtx, k_all, vt_all)
        mixx = _mixer(px, yax, w_pool[l], pool_scale[l], conv_w[l], cb_off)
        ymx = _proj(mixx, w_out, l, d, out_dtype=BF16)

        i = l // 2
        if l % 2 == 0:
            def ffn(stream, ym, row):
                xn, h = _norm(stream, vec(g_ffn), mod(l, row, 3), mod(l, row, 4), resid=(ym, mod(l, row, 2)))
                act = _gate_up(h, w_gate_dense, w_up_dense, i)
                return xn, _down(act, w_down_dense, i)

            xs, yfx = ffn(xs, ymx, 0)
            pend_x = (yfx, mod(l, 0, 5))
            if not last:
                cs, yfc = ffn(cs, ymc, 1)
                pend_c = (yfc, mod(l, 1, 5))
        else:
            wr = jnp.zeros((d, LANES), F32).at[:, :N_EXPERTS].set(w_router[i])
            br = jnp.full((1, LANES), NEG_BIG, F32).at[0, :N_EXPERTS].set(b_router[i])
            xs, hf, ridx, rgw = _norm(xs, vec(g_ffn), mod(l, 0, 3), mod(l, 0, 4), resid=(ymx, mod(l, 0, 2)),
                                      router=(wr, br), mode="route")
            ys, slot_row = _moe_ffn(hf, ridx, w_gate_exp[i], w_up_exp[i], w_down_exp[i])
            y1 = ys.at[slot_row[:, 0]].get(mode="promise_in_bounds")
            y2 = ys.at[slot_row[:, 1]].get(mode="promise_in_bounds")
            pend_x = (y1, y2, rgw, mod(l, 0, 5))

    if len(pend_x) == 4:
        out = _combine_final(xs, pend_x[0], pend_x[1], pend_x[2], pend_x[3], g_final.reshape(1, -1))
    else:
        (out,) = _norm(xs, g_final.reshape(1, -1), resid=pend_x, mode="final")
    return out[None]
```

```python
import functools

import jax
import jax.numpy as jnp
from jax import lax
from jax.experimental import pallas as pl
from jax.experimental.pallas import tpu as pltpu

F32 = jnp.float32
BF16 = jnp.bfloat16

POOL_WINDOWS = (2, 4, 8, 16)
HEAD_DIM = 128
KV_GROUP = 4
N_MOD = 6
N_EXPERTS = 8
TOP_K = 2
GRID_W = 64
ROPE_BASE = 10000.0
EPS = 1e-6

LANES = 128
SUBLANES = 8
VMEM_LIMIT_BYTES = 56 * 1024 * 1024

DISPATCH_CHUNKS = 4
POOL_HALO = 16
CONV_HALO = 8
NEG_BIG = -1e30
LOG2E = 1.4426950408889634


def _params(*sem):
    return pltpu.CompilerParams(dimension_semantics=sem, vmem_limit_bytes=VMEM_LIMIT_BYTES)


def _silu(x):
    return x * jax.nn.sigmoid(x)


def _mods_kernel(c_ref, w_ref, b_ref, o_ref):
    s = _silu(c_ref[...]).astype(BF16)
    w = w_ref[...].astype(BF16)
    o_ref[...] = jnp.dot(s, w, preferred_element_type=F32) + b_ref[...]


def _mods(cvec, w_mod, b_mod, tn=512):
    depth, d, n = w_mod.shape
    return pl.pallas_call(
        _mods_kernel,
        grid=(depth, n // tn),
        in_specs=[
            pl.BlockSpec((SUBLANES, d), lambda l, j: (0, 0)),
            pl.BlockSpec((None, d, tn), lambda l, j: (l, 0, j)),
            pl.BlockSpec((None, 1, tn), lambda l, j: (l, 0, j)),
        ],
        out_specs=pl.BlockSpec((None, SUBLANES, tn), lambda l, j: (l, 0, j)),
        out_shape=jax.ShapeDtypeStruct((depth, SUBLANES, n), F32),
        compiler_params=_params("parallel", "parallel"),
        name="mods",
    )(cvec, w_mod, b_mod.reshape(depth, 1, n))


def _rms(x, g):
    return (x * lax.rsqrt(jnp.mean(x * x, axis=-1, keepdims=True) + EPS)) * g


def _norm_kernel(*refs, has_resid, mode):
    it = iter(refs)
    x_ref = next(it)
    x = x_ref[...]
    if has_resid:
        y_ref, gate_ref = next(it), next(it)
        x = x + gate_ref[...] * y_ref[...].astype(F32)
    g_ref = next(it)
    if mode == "final":
        o_ref = next(it)
        o_ref[...] = _rms(x, g_ref[...])
        return
    shift_ref, scale_ref = next(it), next(it)
    if mode == "route":
        wr_ref, br_ref = next(it), next(it)
    if has_resid:
        xo_ref = next(it)
        xo_ref[...] = x
    h_ref = next(it)
    h = _rms(x, g_ref[...]) * (1.0 + scale_ref[...]) + shift_ref[...]
    h_ref[...] = h.astype(BF16)
    if mode == "route":
        idx_ref, gw_ref = next(it), next(it)
        logits = jnp.dot(h, wr_ref[...], preferred_element_type=F32,
                         precision=lax.Precision.HIGHEST) + br_ref[...]
        lane = lax.broadcasted_iota(jnp.int32, logits.shape, 1).astype(F32)
        m1 = jnp.max(logits, axis=-1, keepdims=True)
        i1 = jnp.min(jnp.where(logits == m1, lane, float(LANES)), axis=-1, keepdims=True)
        rest = jnp.where(lane == i1, -jnp.inf, logits)
        m2 = jnp.max(rest, axis=-1, keepdims=True)
        i2 = jnp.min(jnp.where(rest == m2, lane, float(LANES)), axis=-1, keepdims=True)
        e = jnp.exp(m2 - m1)
        den = 1.0 + e
        idx_ref[...] = jnp.where(lane == 0.0, i1, jnp.where(lane == 1.0, i2, 0.0)).astype(jnp.int32)
        gw_ref[...] = jnp.where(lane == 0.0, 1.0 / den, jnp.where(lane == 1.0, e / den, 0.0))


def _norm(x, g, shift=None, scale=None, resid=None, router=None, mode="mod", tm=256):
    m, d = x.shape
    tm = min(tm, m)
    row = pl.BlockSpec((tm, d), lambda i: (i, 0))
    vec = pl.BlockSpec((1, d), lambda i: (0, 0))
    lane_blk = pl.BlockSpec((tm, LANES), lambda i: (i, 0))
    args, in_specs = [x], [row]
    if resid is not None:
        args += [resid[0], resid[1]]
        in_specs += [row, vec]
    args.append(g)
    in_specs.append(vec)
    out_shape, out_specs = [], []
    if mode != "final":
        args += [shift, scale]
        in_specs += [vec, vec]
        if mode == "route":
            args += [router[0], router[1]]
            in_specs += [pl.BlockSpec((d, LANES), lambda i: (0, 0)), pl.BlockSpec((1, LANES), lambda i: (0, 0))]
        if resid is not None:
            out_shape.append(jax.ShapeDtypeStruct((m, d), F32))
            out_specs.append(row)
        out_shape.append(jax.ShapeDtypeStruct((m, d), BF16))
        out_specs.append(row)
        if mode == "route":
            out_shape += [jax.ShapeDtypeStruct((m, LANES), jnp.int32), jax.ShapeDtypeStruct((m, LANES), F32)]
            out_specs += [lane_blk, lane_blk]
    else:
        out_shape.append(jax.ShapeDtypeStruct((m, d), F32))
        out_specs.append(row)
    return pl.pallas_call(
        functools.partial(_norm_kernel, has_resid=resid is not None, mode=mode),
        grid=(m // tm,),
        in_specs=in_specs,
        out_specs=out_specs,
        out_shape=out_shape,
        compiler_params=_params("parallel"),
        name="norm_" + mode,
    )(*args)


def _proj_kernel(a_ref, w_ref, o_ref):
    o_ref[...] = jnp.dot(a_ref[...], w_ref[...].astype(BF16), preferred_element_type=F32).astype(o_ref.dtype)


def _proj(a, w, layer, n_out, col0=0, tm=1024, tn=512, out_dtype=F32, a_buffers=2):
    m, k = a.shape
    tm = min(tm, m)
    cb = col0 // tn
    a_mode = {} if a_buffers == 2 else {"pipeline_mode": pl.Buffered(a_buffers)}
    return pl.pallas_call(
        _proj_kernel,
        grid=(m // tm, n_out // tn),
        in_specs=[
            pl.BlockSpec((tm, k), lambda i, j: (i, 0), **a_mode),
            pl.BlockSpec((None, k, tn), lambda i, j: (layer, 0, j + cb)),
        ],
        out_specs=pl.BlockSpec((tm, tn), lambda i, j: (i, j)),
        out_shape=jax.ShapeDtypeStruct((m, n_out), out_dtype),
        compiler_params=_params("parallel", "arbitrary"),
        name="proj",
    )(a, w)


def _gate_up_kernel(a_ref, wg_ref, wu_ref, o_ref):
    a = a_ref[...]
    g = jnp.dot(a, wg_ref[...].astype(BF16), preferred_element_type=F32)
    u = jnp.dot(a, wu_ref[...].astype(BF16), preferred_element_type=F32)
    o_ref[...] = (_silu(g) * u).astype(o_ref.dtype)


def _gate_up(a, wg, wu, layer, tm=1024, tn=256):
    m, k = a.shape
    n = wg.shape[-1]
    tm = min(tm, m)
    wspec = pl.BlockSpec((None, k, tn), lambda i, j: (layer, 0, j))
    return pl.pallas_call(
        _gate_up_kernel,
        grid=(m // tm, n // tn),
        in_specs=[pl.BlockSpec((tm, k), lambda i, j: (i, 0)), wspec, wspec],
        out_specs=pl.BlockSpec((tm, tn), lambda i, j: (i, j)),
        out_shape=jax.ShapeDtypeStruct((m, n), BF16),
        compiler_params=_params("parallel", "arbitrary"),
        name="gate_up",
    )(a, wg, wu)


def _qkv_kernel(*refs, want_q):
    if want_q:
        q0_ref, q1_ref, kv_ref, cos_ref, sin_ref, gq_ref, gk_ref, qt_ref, k_ref, vt_ref = refs
    else:
        kv_ref, cos_ref, sin_ref, gk_ref, k_ref, vt_ref = refs
    cos, sin = cos_ref[...], sin_ref[...]
    tm = cos.shape[0]

    def head(x, g):
        y = _rms(x, g)
        return y * cos + pltpu.roll(y, HEAD_DIM // 2, axis=1) * sin

    if want_q:
        heads_per_blk = q0_ref.shape[1] // HEAD_DIM
        for b, src in enumerate((q0_ref, q1_ref)):
            for hh in range(heads_per_blk):
                lo = hh * HEAD_DIM
                kv, g = divmod(b * heads_per_blk + hh, KV_GROUP)
                qh = head(src[:, lo:lo + HEAD_DIM], gq_ref[...])
                qt_ref[kv, :, g * tm:(g + 1) * tm] = qh.T.astype(BF16)
    kvw = k_ref.shape[1]
    for hh in range(kvw // HEAD_DIM):
        lo = hh * HEAD_DIM
        k_ref[:, lo:lo + HEAD_DIM] = head(kv_ref[:, lo:lo + HEAD_DIM], gk_ref[...]).astype(BF16)
    vt_ref[...] = kv_ref[:, kvw:].T.astype(BF16)


def _qkv_prep(p, cosf, sinf, gq_scaled, gk, q_col0, kv_col0, tm=512):
    m = p.shape[0]
    tm = min(tm, m)
    blk = 2 * KV_GROUP * HEAD_DIM
    kvw = blk // 2
    hkv = kvw // HEAD_DIM
    want_q = q_col0 is not None
    pspec = lambda c: pl.BlockSpec((tm, blk), lambda i: (i, c))
    tab = pl.BlockSpec((tm, HEAD_DIM), lambda i: (i, 0))
    gvec = pl.BlockSpec((1, HEAD_DIM), lambda i: (0, 0))
    kv_specs = [pl.BlockSpec((tm, kvw), lambda i: (i, 0)), pl.BlockSpec((kvw, tm), lambda i: (0, i))]
    kv_shapes = [jax.ShapeDtypeStruct((m, kvw), BF16), jax.ShapeDtypeStruct((kvw, m), BF16)]
    if want_q:
        args = (p, p, p, cosf, sinf, gq_scaled, gk)
        in_specs = [pspec(q_col0 // blk), pspec(q_col0 // blk + 1), pspec(kv_col0 // blk), tab, tab, gvec, gvec]
        out_specs = [pl.BlockSpec((hkv, None, HEAD_DIM, KV_GROUP * tm), lambda i: (0, i, 0, 0))] + kv_specs
        out_shape = [jax.ShapeDtypeStruct((hkv, m // tm, HEAD_DIM, KV_GROUP * tm), BF16)] + kv_shapes
    else:
        args = (p, cosf, sinf, gk)
        in_specs = [pspec(kv_col0 // blk), tab, tab, gvec]
        out_specs = kv_specs
        out_shape = kv_shapes
    return pl.pallas_call(
        functools.partial(_qkv_kernel, want_q=want_q),
        grid=(m // tm,),
        in_specs=in_specs,
        out_specs=out_specs,
        out_shape=out_shape,
        compiler_params=_params("parallel"),
        name="qkv_prep",
    )(*args)


def _flash_t_kernel(qn_ref, kn_ref, qc_ref, kc_ref, vt_ref, o_ref,
                    s_a, s_b, mx_a, mx_b, m_ref, l_ref, acc_ref):
    h, i, j = pl.program_id(0), pl.program_id(1), pl.program_id(2)
    nq, nk = pl.num_programs(1), pl.num_programs(2)
    step = (h * nq + i) * nk + j

    def scores(k_ref, q_ref, s_ref, mx_ref):
        s = jnp.dot(k_ref[...], q_ref[...], preferred_element_type=F32)
        s_ref[...] = s
        mx_ref[...] = jnp.max(s, axis=0, keepdims=True)

    @pl.when(step == 0)
    def _():
        scores(kc_ref, qc_ref, s_a, mx_a)

    @pl.when(j == 0)
    def _():
        m_ref[...] = jnp.full(m_ref.shape, -jnp.inf, F32)
        l_ref[...] = jnp.zeros(l_ref.shape, F32)
        acc_ref[...] = jnp.zeros(acc_ref.shape, F32)

    def body(s_cur, mx_cur, s_nxt, mx_nxt):
        scores(kn_ref, qn_ref, s_nxt, mx_nxt)
        m_prev = m_ref[...]
        m_new = jnp.maximum(m_prev, mx_cur[...])
        alpha = jnp.exp2(m_prev - m_new)
        p = jnp.exp2(s_cur[...] - m_new)
        l_ref[...] = alpha * l_ref[...] + jnp.sum(p, axis=0, keepdims=True)
        acc_ref[...] = alpha * acc_ref[...] + jnp.dot(vt_ref[...], p.astype(BF16), preferred_element_type=F32)
        m_ref[...] = m_new

    @pl.when(step % 2 == 0)
    def _():
        body(s_a, mx_a, s_b, mx_b)

    @pl.when(step % 2 == 1)
    def _():
        body(s_b, mx_b, s_a, mx_a)

    @pl.when(j == nk - 1)
    def _():
        tq = o_ref.shape[0]
        o = acc_ref[...] / l_ref[...]
        for g in range(KV_GROUP):
            o_ref[:, g * HEAD_DIM:(g + 1) * HEAD_DIM] = o[:, g * tq:(g + 1) * tq].T.astype(o_ref.dtype)


def _flash_t(qt, k, vt, tk=1408):
    hkv, nq, _, nqc = qt.shape
    tq = nqc // KV_GROUP
    lk = k.shape[0]
    tk = min(tk, lk)
    nk = lk // tk
    gw = KV_GROUP * HEAD_DIM

    def nxt(h, i, j):
        j1 = j + 1
        i1 = i + j1 // nk
        h1 = h + i1 // nq
        return jnp.minimum(h1, hkv - 1), i1 % nq, j1 % nk

    def qn_map(h, i, j):
        h1, i1, _ = nxt(h, i, j)
        return h1, i1, 0, 0

    def kn_map(h, i, j):
        h1, _, j1 = nxt(h, i, j)
        return j1, h1

    qspec = lambda m: pl.BlockSpec((None, None, HEAD_DIM, nqc), m)
    return pl.pallas_call(
        _flash_t_kernel,
        grid=(hkv, nq, nk),
        in_specs=[
            qspec(qn_map),
            pl.BlockSpec((tk, HEAD_DIM), kn_map),
            qspec(lambda h, i, j: (h, i, 0, 0)),
            pl.BlockSpec((tk, HEAD_DIM), lambda h, i, j: (0, h)),
            pl.BlockSpec((HEAD_DIM, tk), lambda h, i, j: (h, j)),
        ],
        out_specs=pl.BlockSpec((tq, gw), lambda h, i, j: (i, h)),
        out_shape=jax.ShapeDtypeStruct((nq * tq, hkv * gw), BF16),
        scratch_shapes=[pltpu.VMEM((tk, nqc), F32), pltpu.VMEM((tk, nqc), F32),
                        pltpu.VMEM((1, nqc), F32), pltpu.VMEM((1, nqc), F32),
                        pltpu.VMEM((1, nqc), F32), pltpu.VMEM((1, nqc), F32),
                        pltpu.VMEM((HEAD_DIM, nqc), F32)],
        compiler_params=_params("arbitrary", "arbitrary", "arbitrary"),
        name="flash_t",
    )(qt, k, qt, k, vt)


def _mixer_kernel(pc_ref, pp_ref, pn_ref, cb_ref, cc_ref, ccp_ref, ccn_ref, cx_ref, cxp_ref, cxn_ref,
                  ya_ref, wp_ref, ps_ref, cw_ref, o_ref, *, seq_len):
    i = pl.program_id(0)
    last = pl.num_programs(0) - 1
    tm = pc_ref.shape[0]
    n_ext = tm + 2 * POOL_HALO
    pool_w = pc_ref.shape[1]
    grp = pool_w // len(POOL_WINDOWS)

    prev = jnp.where(i == 0, 0.0, pp_ref[...])
    nxt = jnp.where(i == last, 0.0, pn_ref[...])
    ext = jnp.concatenate([prev, pc_ref[...], nxt], axis=0)
    t = (i * tm + lax.broadcasted_iota(jnp.int32, (tm, 1), 0))

    def shift_up(x, k):
        return pltpu.roll(x, n_ext - k, axis=0)

    def shift_down(x, k):
        return pltpu.roll(x, k, axis=0)

    for gi, w in enumerate(POOL_WINDOWS):
        e = ext[:, gi * grp:(gi + 1) * grp]
        half = w // 2
        part, span = e, 1
        while span < half:
            part = part + shift_up(part, span)
            span *= 2
        win = shift_down(part, half) + part
        win = win[POOL_HALO:POOL_HALO + tm]
        lo = jnp.clip(t - half, 0, seq_len)
        hi = jnp.clip(t - half + w, 0, seq_len)
        cnt = (hi - lo).astype(F32)
        pooled = win / cnt - e[POOL_HALO:POOL_HALO + tm]
        y = jnp.dot(pooled.astype(BF16), wp_ref[gi].astype(BF16), preferred_element_type=F32)
        o_ref[:, gi * grp:(gi + 1) * grp] = (y * ps_ref[:, gi * grp:(gi + 1) * grp]).astype(o_ref.dtype)

    aw = ya_ref.shape[1]
    o_ref[:, pool_w:pool_w + aw] = ya_ref[...]

    u = cc_ref[...] * cx_ref[...]
    u_before = jnp.where(i == 0, 0.0, ccp_ref[CONV_HALO - 1:CONV_HALO, :] * cxp_ref[CONV_HALO - 1:CONV_HALO, :])
    u_after = jnp.where(i == last, 0.0, ccn_ref[0:1, :] * cxn_ref[0:1, :])
    r = lax.broadcasted_iota(jnp.int32, (tm, 1), 0)
    um1 = jnp.where(r == 0, u_before, pltpu.roll(u, 1, axis=0))
    up1 = jnp.where(r == tm - 1, u_after, pltpu.roll(u, tm - 1, axis=0))
    conv = cw_ref[0:1, :] * um1 + cw_ref[1:2, :] * u + cw_ref[2:3, :] * up1
    o_ref[:, pool_w + aw:] = (cb_ref[...] * conv).astype(o_ref.dtype)


def _mixer(p, ya, w_pool, pool_scale, conv_w, cb_col0, tm=256):
    m = p.shape[0]
    tm = min(tm, m)
    blk = pool_scale.shape[-1]
    c0 = cb_col0 // blk
    nblk_pool, nblk_conv = m // POOL_HALO, m // CONV_HALO
    rp, rc = tm // POOL_HALO, tm // CONV_HALO

    cur = lambda c: pl.BlockSpec((tm, blk), lambda i: (i, c))
    prev_p = pl.BlockSpec((POOL_HALO, blk), lambda i: (jnp.maximum(i * rp - 1, 0), 0))
    next_p = pl.BlockSpec((POOL_HALO, blk), lambda i: (jnp.minimum((i + 1) * rp, nblk_pool - 1), 0))
    prev_c = lambda c: pl.BlockSpec((CONV_HALO, blk), lambda i: (jnp.maximum(i * rc - 1, 0), c))
    next_c = lambda c: pl.BlockSpec((CONV_HALO, blk), lambda i: (jnp.minimum((i + 1) * rc, nblk_conv - 1), c))
    aw = ya.shape[1]
    full = lambda a: pl.BlockSpec(a.shape, lambda i: (0,) * a.ndim)
    ps = pool_scale.reshape(1, blk)
    return pl.pallas_call(
        functools.partial(_mixer_kernel, seq_len=m),
        grid=(m // tm,),
        in_specs=[cur(0), prev_p, next_p,
                  cur(c0), cur(c0 + 1), prev_c(c0 + 1), next_c(c0 + 1),
                  cur(c0 + 2), prev_c(c0 + 2), next_c(c0 + 2),
                  pl.BlockSpec((tm, aw), lambda i: (i, 0)),
                  full(w_pool), full(ps), full(conv_w)],
        out_specs=pl.BlockSpec((tm, 2 * blk + aw), lambda i: (i, 0)),
        out_shape=jax.ShapeDtypeStruct((m, 2 * blk + aw), BF16),
        compiler_params=_params("parallel"),
        name="mixer",
    )(p, p, p, p, p, p, p, p, p, p, ya, w_pool, ps, conv_w)


def _down(act, wd, layer, tm=1024, tn=256):
    return _proj(act, wd, layer, wd.shape[-1], tm=tm, tn=tn, a_buffers=1, out_dtype=BF16)


def _moe_tile_kernel(te_ref, nv_ref, tr_ref, a_ref, *refs, body, n_w):
    w_refs, o_ref = refs[:n_w], refs[-1]
    tm = o_ref.shape[0]
    half = tm // 2
    rows = tr_ref[pl.program_id(1)]

    @pl.when(rows > half)
    def _():
        body(a_ref, *w_refs, o_ref)

    @pl.when(jnp.logical_and(rows > 0, rows <= half))
    def _():
        body(a_ref.at[pl.ds(0, half)], *w_refs, o_ref.at[pl.ds(0, half)])
        o_ref[pl.ds(half, tm - half), :] = jnp.zeros((tm - half, o_ref.shape[1]), o_ref.dtype)

    @pl.when(rows == 0)
    def _():
        o_ref[...] = jnp.zeros(o_ref.shape, o_ref.dtype)


_moe_gate_up_kernel = functools.partial(_moe_tile_kernel, body=_gate_up_kernel, n_w=2)
_moe_down_kernel = functools.partial(_moe_tile_kernel, body=_proj_kernel, n_w=1)


def _moe_call(kernel, a, weights, tile_expert, n_valid, tile_rows, n_out, out_dtype, tm, tn, name,
              prev=None, tile0=0, r_total=None):
    r, k = a.shape
    n_tiles = r // tm
    row = lambda j, t, te, nv, tr: (jnp.clip(t, 0, jnp.maximum(nv[0] - 1, 0)), 0)
    wspec = pl.BlockSpec((None, k, tn), lambda j, t, te, nv, tr: (te[t], 0, j))
    in_specs = [pl.BlockSpec((tm, k), row)] + [wspec] * len(weights)
    args = [tile_expert, n_valid, tile_rows, a, *weights]
    aliases = {}
    if prev is not None:
        in_specs.append(pl.BlockSpec(memory_space=pl.ANY))
        aliases = {len(args): 0}
        args.append(prev)
    return pl.pallas_call(
        kernel,
        grid_spec=pltpu.PrefetchScalarGridSpec(
            num_scalar_prefetch=3,
            grid=(n_out // tn, n_tiles),
            in_specs=in_specs,
            out_specs=pl.BlockSpec((tm, tn), lambda j, t, te, nv, tr: (t + tile0, j)),
        ),
        out_shape=jax.ShapeDtypeStruct((r_total or r, n_out), out_dtype),
        input_output_aliases=aliases,
        compiler_params=_params("parallel", "arbitrary"),
        name=name,
    )(*args)


def _moe_ffn(h, ridx, wg, wu, wd, tm=512, tn=512, tn_down=1024):
    m, d = h.shape
    n_slots = m * TOP_K
    r_max = n_slots + N_EXPERTS * tm
    n_tiles = r_max // tm
    e_flat = ridx[:, :TOP_K].reshape(n_slots)
    order = jnp.argsort(e_flat, stable=True).astype(jnp.int32)
    counts = jnp.sum((e_flat[:, None] == jnp.arange(N_EXPERTS)[None, :]).astype(jnp.int32), axis=0)
    padded = ((counts + tm - 1) // tm) * tm
    gend = jnp.cumsum(padded)
    gstart = gend - padded
    cstart = jnp.cumsum(counts) - counts
    n_valid = (gend[-1] // tm).astype(jnp.int32).reshape(1)
    tile_expert = jnp.sum((gend[None, :] <= (jnp.arange(n_tiles) * tm)[:, None]).astype(jnp.int32), axis=1)
    tile_expert = jnp.minimum(tile_expert, N_EXPERTS - 1)
    last_e = tile_expert[jnp.maximum(n_valid[0] - 1, 0)]
    tile_expert = jnp.where(jnp.arange(n_tiles) < n_valid[0], tile_expert, last_e)
    rows = jnp.arange(r_max)
    row_e = tile_expert[rows // tm]
    off = rows - gstart[row_e]
    row_valid = (off < counts[row_e]) & (rows < gend[-1])
    src = jnp.clip(cstart[row_e] + off, 0, n_slots - 1)
    row_token = jnp.where(row_valid, order[src] // TOP_K, rows % m)
    tile_lo = jnp.arange(n_tiles) * tm
    tile_rows = jnp.clip(gstart[tile_expert] + counts[tile_expert] - tile_lo, 0, tm)
    tile_rows = jnp.where(jnp.arange(n_tiles) < n_valid[0], tile_rows, 0).astype(jnp.int32)
    rank = jnp.argsort(order).astype(jnp.int32)
    slot_row = (gstart[e_flat] + rank - cstart[e_flat]).reshape(m, TOP_K)

    assert n_tiles % DISPATCH_CHUNKS == 0
    ct = n_tiles // DISPATCH_CHUNKS
    act = None
    for c in range(DISPATCH_CHUNKS):
        t0 = c * ct
        hs = h.at[row_token[t0 * tm:(t0 + ct) * tm]].get(mode="promise_in_bounds")
        nv_c = jnp.clip(n_valid - t0, 0, ct)
        act = _moe_call(_moe_gate_up_kernel, hs, (wg, wu), tile_expert[t0:t0 + ct], nv_c, tile_rows[t0:t0 + ct],
                        wg.shape[-1], BF16, tm, tn, "moe_gate_up", prev=act, tile0=t0, r_total=r_max)
    ys = _moe_call(_moe_down_kernel, act, (wd,), tile_expert, n_valid, tile_rows, d, BF16, tm, tn_down,
                   "moe_down")
    return ys, slot_row


def _combine_kernel(x_ref, y1_ref, y2_ref, gw_ref, gate_ref, g_ref, o_ref):
    gw = gw_ref[...]
    y = gw[:, 0:1] * y1_ref[...].astype(F32) + gw[:, 1:2] * y2_ref[...].astype(F32)
    o_ref[...] = _rms(x_ref[...] + gate_ref[...] * y, g_ref[...])


def _combine_final(x, y1, y2, gw, gate, g, tm=256):
    m, d = x.shape
    row = pl.BlockSpec((tm, d), lambda i: (i, 0))
    vec = pl.BlockSpec((1, d), lambda i: (0, 0))
    return pl.pallas_call(
        _combine_kernel,
        grid=(m // tm,),
        in_specs=[row, row, row, pl.BlockSpec((tm, LANES), lambda i: (i, 0)), vec, vec],
        out_specs=row,
        out_shape=jax.ShapeDtypeStruct((m, d), F32),
        compiler_params=_params("parallel"),
        name="combine_final",
    )(x, y1, y2, gw, gate, g)


def _rope_tables(seq_len):
    rows = seq_len // GRID_W
    row = jnp.repeat(jnp.arange(rows), GRID_W).astype(F32)
    col = jnp.tile(jnp.arange(GRID_W), rows).astype(F32)
    n_axis = HEAD_DIM // 4
    inv = ROPE_BASE ** (-jnp.arange(n_axis, dtype=F32) / n_axis)
    ang = jnp.concatenate([row[:, None] * inv, col[:, None] * inv], axis=-1)
    cos, sin = jnp.cos(ang), jnp.sin(ang)
    return jnp.concatenate([cos, cos], axis=-1), jnp.concatenate([-sin, sin], axis=-1)


def kernel(x, c, ctx, c_ctx, w_mod, b_mod, g_mix, w_in, w_pool, pool_scale, g_q, g_k, conv_w, w_out, g_ffn,
           w_gate_dense, w_up_dense, w_down_dense, w_router, b_router, w_gate_exp, w_up_exp, w_down_exp, g_final):
    batch, seq, d = x.shape
    depth = w_mod.shape[0]
    assert batch == 1 and depth == 2
    n_ctx = ctx.shape[1]
    pool_w = pool_scale.shape[-1]
    conv_wd = conv_w.shape[-1]
    q_w = (d // (2 * HEAD_DIM)) * HEAD_DIM
    kv_w = q_w // KV_GROUP
    q_off = pool_w
    k_off = q_off + q_w
    cb_off = k_off + 2 * kv_w
    in_w = cb_off + 3 * conv_wd

    xs, cs = x[0], ctx[0]
    cvec = jnp.zeros((SUBLANES, d), F32).at[0].set(c[0]).at[1].set(c_ctx)
    mods = _mods(cvec, w_mod, b_mod)

    def mod(l, row, k):
        return mods[l, row:row + 1, k * d:(k + 1) * d]

    cosf, sinf = _rope_tables(seq)
    cos_c, sin_c = jnp.ones((n_ctx, HEAD_DIM), F32), jnp.zeros((n_ctx, HEAD_DIM), F32)
    qscale = HEAD_DIM ** -0.5 * LOG2E

    pend_x = pend_c = None
    for l in range(depth):
        last = l == depth - 1
        vec = lambda a: a[l].reshape(1, -1)
        gq, gk = vec(g_q) * qscale, vec(g_k)

        def first_norm(stream, pend, row):
            if pend is None:
                (h,) = _norm(stream, vec(g_mix), mod(l, row, 0), mod(l, row, 1))
                return stream, h
            return _norm(stream, vec(g_mix), mod(l, row, 0), mod(l, row, 1), resid=pend)

        cs, hc = first_norm(cs, pend_c, 1)
        if last:
            pkv = _proj(hc, w_in, l, 2 * kv_w, col0=k_off)
            kc, vtc = _qkv_prep(pkv, cos_c, sin_c, gq, gk, None, 0)
        else:
            pc = _proj(hc, w_in, l, in_w)
            qtc, kc, vtc = _qkv_prep(pc, cos_c, sin_c, gq, gk, q_off, k_off)
            yac = _flash_t(qtc, kc, vtc)
            mixc = _mixer(pc, yac, w_pool[l], pool_scale[l], conv_w[l], cb_off)
            ymc = _proj(mixc, w_out, l, d, out_dtype=BF16)

        xs, hx = first_norm(xs, pend_x, 0)
        px = _proj(hx, w_in, l, in_w)
        qtx, kx, vtx = _qkv_prep(px, cosf, sinf, gq, gk, q_off, k_off)
        k_all = jnp.concatenate([kc, kx], axis=0)
        vt_all = jnp.concatenate([vtc, vtx], axis=1)
        yax = _flash_t(qtx, k_all, vt_all)
        mixx = _mixer(px, yax, w_pool[l], pool_scale[l], conv_w[l], cb_off)
        ymx = _proj(mixx, w_out, l, d, out_dtype=BF16)

        i = l // 2
        if l % 2 == 0:
            def ffn(stream, ym, row):
                xn, h = _norm(stream, vec(g_ffn), mod(l, row, 3), mod(l, row, 4), resid=(ym, mod(l, row, 2)))
                act = _gate_up(h, w_gate_dense, w_up_dense, i)
                return xn, _down(act, w_down_dense, i)

            xs, yfx = ffn(xs, ymx, 0)
            pend_x = (yfx, mod(l, 0, 5))
            if not last:
                cs, yfc = ffn(cs, ymc, 1)
                pend_c = (yfc, mod(l, 1, 5))
        else:
            wr = jnp.zeros((d, LANES), F32).at[:, :N_EXPERTS].set(w_router[i])
            br = jnp.full((1, LANES), NEG_BIG, F32).at[0, :N_EXPERTS].set(b_router[i])
            xs, hf, ridx, rgw = _norm(xs, vec(g_ffn), mod(l, 0, 3), mod(l, 0, 4), resid=(ymx, mod(l, 0, 2)),
                                      router=(wr, br), mode="route")
            ys, slot_row = _moe_ffn(hf, ridx, w_gate_exp[i], w_up_exp[i], w_down_exp[i])
            y1 = ys.at[slot_row[:, 0]].get(mode="promise_in_bounds")
            y2 = ys.at[slot_row[:, 1]].get(mode="promise_in_bounds")
            pend_x = (y1, y2, rgw, mod(l, 0, 5))

    if len(pend_x) == 4:
        out = _combine_final(xs, pend_x[0], pend_x[1], pend_x[2], pend_x[3], g_final.reshape(1, -1))
    else:
        (out,) = _norm(xs, g_final.reshape(1, -1), resid=pend_x, mode="final")
    return out[None]
```

```python
import functools

import jax
import jax.numpy as jnp
from jax import lax
from jax.experimental import pallas as pl
from jax.experimental.pallas import tpu as pltpu

F32 = jnp.float32
BF16 = jnp.bfloat16

POOL_WINDOWS = (2, 4, 8, 16)
HEAD_DIM = 128
KV_GROUP = 4
N_MOD = 6
N_EXPERTS = 8
TOP_K = 2
GRID_W = 64
ROPE_BASE = 10000.0
EPS = 1e-6

LANES = 128
SUBLANES = 8
VMEM_LIMIT_BYTES = 56 * 1024 * 1024

DISPATCH_FIRST_DIV = 5
POOL_HALO = 16
CONV_HALO = 8
NEG_BIG = -1e30
LOG2E = 1.4426950408889634


def _params(*sem):
    return pltpu.CompilerParams(dimension_semantics=sem, vmem_limit_bytes=VMEM_LIMIT_BYTES)


def _silu(x):
    return x * jax.nn.sigmoid(x)


def _mods_kernel(c_ref, w_ref, b_ref, o_ref):
    s = _silu(c_ref[...]).astype(BF16)
    w = w_ref[...].astype(BF16)
    o_ref[...] = jnp.dot(s, w, preferred_element_type=F32) + b_ref[...]


def _mods(cvec, w_mod, b_mod, tn=512):
    depth, d, n = w_mod.shape
    return pl.pallas_call(
        _mods_kernel,
        grid=(depth, n // tn),
        in_specs=[
            pl.BlockSpec((SUBLANES, d), lambda l, j: (0, 0)),
            pl.BlockSpec((None, d, tn), lambda l, j: (l, 0, j)),
            pl.BlockSpec((None, 1, tn), lambda l, j: (l, 0, j)),
        ],
        out_specs=pl.BlockSpec((None, SUBLANES, tn), lambda l, j: (l, 0, j)),
        out_shape=jax.ShapeDtypeStruct((depth, SUBLANES, n), F32),
        compiler_params=_params("parallel", "parallel"),
        name="mods",
    )(cvec, w_mod, b_mod.reshape(depth, 1, n))


def _rms(x, g):
    return (x * lax.rsqrt(jnp.mean(x * x, axis=-1, keepdims=True) + EPS)) * g


def _norm_kernel(*refs, has_resid, mode):
    it = iter(refs)
    x_ref = next(it)
    x = x_ref[...]
    if has_resid:
        y_ref, gate_ref = next(it), next(it)
        x = x + gate_ref[...] * y_ref[...].astype(F32)
    g_ref = next(it)
    if mode == "final":
        o_ref = next(it)
        o_ref[...] = _rms(x, g_ref[...])
        return
    shift_ref, scale_ref = next(it), next(it)
    if mode == "route":
        wr_ref, br_ref = next(it), next(it)
    if has_resid:
        xo_ref = next(it)
        xo_ref[...] = x
    h_ref = next(it)
    h = _rms(x, g_ref[...]) * (1.0 + scale_ref[...]) + shift_ref[...]
    h_ref[...] = h.astype(BF16)
    if mode == "route":
        idx_ref, gw_ref = next(it), next(it)
        logits = jnp.dot(h, wr_ref[...], preferred_element_type=F32,
                         precision=lax.Precision.HIGHEST) + br_ref[...]
        lane = lax.broadcasted_iota(jnp.int32, logits.shape, 1).astype(F32)
        m1 = jnp.max(logits, axis=-1, keepdims=True)
        i1 = jnp.min(jnp.where(logits == m1, lane, float(LANES)), axis=-1, keepdims=True)
        rest = jnp.where(lane == i1, -jnp.inf, logits)
        m2 = jnp.max(rest, axis=-1, keepdims=True)
        i2 = jnp.min(jnp.where(rest == m2, lane, float(LANES)), axis=-1, keepdims=True)
        e = jnp.exp(m2 - m1)
        den = 1.0 + e
        idx_ref[...] = jnp.where(lane == 0.0, i1, jnp.where(lane == 1.0, i2, 0.0)).astype(jnp.int32)
        gw_ref[...] = jnp.where(lane == 0.0, 1.0 / den, jnp.where(lane == 1.0, e / den, 0.0))


def _norm(x, g, shift=None, scale=None, resid=None, router=None, mode="mod", tm=256):
    m, d = x.shape
    tm = min(tm, m)
    row = pl.BlockSpec((tm, d), lambda i: (i, 0))
    vec = pl.BlockSpec((1, d), lambda i: (0, 0))
    lane_blk = pl.BlockSpec((tm, LANES), lambda i: (i, 0))
    args, in_specs = [x], [row]
    if resid is not None:
        args += [resid[0], resid[1]]
        in_specs += [row, vec]
    args.append(g)
    in_specs.append(vec)
    out_shape, out_specs = [], []
    if mode != "final":
        args += [shift, scale]
        in_specs += [vec, vec]
        if mode == "route":
            args += [router[0], router[1]]
            in_specs += [pl.BlockSpec((d, LANES), lambda i: (0, 0)), pl.BlockSpec((1, LANES), lambda i: (0, 0))]
        if resid is not None:
            out_shape.append(jax.ShapeDtypeStruct((m, d), F32))
            out_specs.append(row)
        out_shape.append(jax.ShapeDtypeStruct((m, d), BF16))
        out_specs.append(row)
        if mode == "route":
            out_shape += [jax.ShapeDtypeStruct((m, LANES), jnp.int32), jax.ShapeDtypeStruct((m, LANES), F32)]
            out_specs += [lane_blk, lane_blk]
    else:
        out_shape.append(jax.ShapeDtypeStruct((m, d), F32))
        out_specs.append(row)
    return pl.pallas_call(
        functools.partial(_norm_kernel, has_resid=resid is not None, mode=mode),
        grid=(m // tm,),
        in_specs=in_specs,
        out_specs=out_specs,
        out_shape=out_shape,
        compiler_params=_params("parallel"),
        name="norm_" + mode,
    )(*args)


def _proj_kernel(a_ref, w_ref, o_ref):
    o_ref[...] = jnp.dot(a_ref[...], w_ref[...].astype(BF16), preferred_element_type=F32).astype(o_ref.dtype)


def _proj(a, w, layer, n_out, col0=0, tm=1024, tn=512, out_dtype=F32, a_buffers=2):
    m, k = a.shape
    tm = min(tm, m)
    cb = col0 // tn
    a_mode = {} if a_buffers == 2 else {"pipeline_mode": pl.Buffered(a_buffers)}
    return pl.pallas_call(
        _proj_kernel,
        grid=(m // tm, n_out // tn),
        in_specs=[
            pl.BlockSpec((tm, k), lambda i, j: (i, 0), **a_mode),
            pl.BlockSpec((None, k, tn), lambda i, j: (layer, 0, j + cb)),
        ],
        out_specs=pl.BlockSpec((tm, tn), lambda i, j: (i, j)),
        out_shape=jax.ShapeDtypeStruct((m, n_out), out_dtype),
        compiler_params=_params("parallel", "arbitrary"),
        name="proj",
    )(a, w)


def _gate_up_kernel(a_ref, wg_ref, wu_ref, o_ref):
    a = a_ref[...]
    g = jnp.dot(a, wg_ref[...].astype(BF16), preferred_element_type=F32)
    u = jnp.dot(a, wu_ref[...].astype(BF16), preferred_element_type=F32)
    o_ref[...] = (_silu(g) * u).astype(o_ref.dtype)


def _gate_up(a, wg, wu, layer, tm=1024, tn=256):
    m, k = a.shape
    n = wg.shape[-1]
    tm = min(tm, m)
    wspec = pl.BlockSpec((None, k, tn), lambda i, j: (layer, 0, j))
    return pl.pallas_call(
        _gate_up_kernel,
        grid=(m // tm, n // tn),
        in_specs=[pl.BlockSpec((tm, k), lambda i, j: (i, 0)), wspec, wspec],
        out_specs=pl.BlockSpec((tm, tn), lambda i, j: (i, j)),
        out_shape=jax.ShapeDtypeStruct((m, n), BF16),
        compiler_params=_params("parallel", "arbitrary"),
        name="gate_up",
    )(a, wg, wu)


def _qkv_kernel(*refs, want_q):
    if want_q:
        q0_ref, q1_ref, kv_ref, cos_ref, sin_ref, gq_ref, gk_ref, qt_ref, k_ref, vt_ref = refs
    else:
        kv_ref, cos_ref, sin_ref, gk_ref, k_ref, vt_ref = refs
    cos, sin = cos_ref[...], sin_ref[...]
    tm = cos.shape[0]

    def head(x, g):
        y = _rms(x, g)
        return y * cos + pltpu.roll(y, HEAD_DIM // 2, axis=1) * sin

    if want_q:
        heads_per_blk = q0_ref.shape[1] // HEAD_DIM
        for b, src in enumerate((q0_ref, q1_ref)):
            for hh in range(heads_per_blk):
                lo = hh * HEAD_DIM
                kv, g = divmod(b * heads_per_blk + hh, KV_GROUP)
                qh = head(src[:, lo:lo + HEAD_DIM], gq_ref[...])
                qt_ref[kv, :, g * tm:(g + 1) * tm] = qh.T.astype(BF16)
    kvw = k_ref.shape[1]
    for hh in range(kvw // HEAD_DIM):
        lo = hh * HEAD_DIM
        k_ref[:, lo:lo + HEAD_DIM] = head(kv_ref[:, lo:lo + HEAD_DIM], gk_ref[...]).astype(BF16)
    vt_ref[...] = kv_ref[:, kvw:].T.astype(BF16)


def _qkv_prep(p, cosf, sinf, gq_scaled, gk, q_col0, kv_col0, tm=512):
    m = p.shape[0]
    tm = min(tm, m)
    blk = 2 * KV_GROUP * HEAD_DIM
    kvw = blk // 2
    hkv = kvw // HEAD_DIM
    want_q = q_col0 is not None
    pspec = lambda c: pl.BlockSpec((tm, blk), lambda i: (i, c))
    tab = pl.BlockSpec((tm, HEAD_DIM), lambda i: (i, 0))
    gvec = pl.BlockSpec((1, HEAD_DIM), lambda i: (0, 0))
    kv_specs = [pl.BlockSpec((tm, kvw), lambda i: (i, 0)), pl.BlockSpec((kvw, tm), lambda i: (0, i))]
    kv_shapes = [jax.ShapeDtypeStruct((m, kvw), BF16), jax.ShapeDtypeStruct((kvw, m), BF16)]
    if want_q:
        args = (p, p, p, cosf, sinf, gq_scaled, gk)
        in_specs = [pspec(q_col0 // blk), pspec(q_col0 // blk + 1), pspec(kv_col0 // blk), tab, tab, gvec, gvec]
        out_specs = [pl.BlockSpec((hkv, None, HEAD_DIM, KV_GROUP * tm), lambda i: (0, i, 0, 0))] + kv_specs
        out_shape = [jax.ShapeDtypeStruct((hkv, m // tm, HEAD_DIM, KV_GROUP * tm), BF16)] + kv_shapes
    else:
        args = (p, cosf, sinf, gk)
        in_specs = [pspec(kv_col0 // blk), tab, tab, gvec]
        out_specs = kv_specs
        out_shape = kv_shapes
    return pl.pallas_call(
        functools.partial(_qkv_kernel, want_q=want_q),
        grid=(m // tm,),
        in_specs=in_specs,
        out_specs=out_specs,
        out_shape=out_shape,
        compiler_params=_params("parallel"),
        name="qkv_prep",
    )(*args)


def _flash_t_kernel(qn_ref, kn_ref, qc_ref, kc_ref, vt_ref, o_ref,
                    s_a, s_b, mx_a, mx_b, m_ref, l_ref, acc_ref):
    h, i, j = pl.program_id(0), pl.program_id(1), pl.program_id(2)
    nq, nk = pl.num_programs(1), pl.num_programs(2)
    step = (h * nq + i) * nk + j

    def scores(k_ref, q_ref, s_ref, mx_ref):
        s = jnp.dot(k_ref[...], q_ref[...], preferred_element_type=F32)
        s_ref[...] = s
        mx_ref[...] = jnp.max(s, axis=0, keepdims=True)

    @pl.when(step == 0)
    def _():
        scores(kc_ref, qc_ref, s_a, mx_a)

    @pl.when(j == 0)
    def _():
        m_ref[...] = jnp.full(m_ref.shape, -jnp.inf, F32)
        l_ref[...] = jnp.zeros(l_ref.shape, F32)
        acc_ref[...] = jnp.zeros(acc_ref.shape, F32)

    def body(s_cur, mx_cur, s_nxt, mx_nxt):
        scores(kn_ref, qn_ref, s_nxt, mx_nxt)
        m_prev = m_ref[...]
        m_new = jnp.maximum(m_prev, mx_cur[...])
        alpha = jnp.exp2(m_prev - m_new)
        p = jnp.exp2(s_cur[...] - m_new)
        l_ref[...] = alpha * l_ref[...] + jnp.sum(p, axis=0, keepdims=True)
        acc_ref[...] = alpha * acc_ref[...] + jnp.dot(vt_ref[...], p.astype(BF16), preferred_element_type=F32)
        m_ref[...] = m_new

    @pl.when(step % 2 == 0)
    def _():
        body(s_a, mx_a, s_b, mx_b)

    @pl.when(step % 2 == 1)
    def _():
        body(s_b, mx_b, s_a, mx_a)

    @pl.when(j == nk - 1)
    def _():
        tq = o_ref.shape[0]
        o = acc_ref[...] / l_ref[...]
        for g in range(KV_GROUP):
            o_ref[:, g * HEAD_DIM:(g + 1) * HEAD_DIM] = o[:, g * tq:(g + 1) * tq].T.astype(o_ref.dtype)


def _flash_t(qt, k, vt, tk=1408):
    hkv, nq, _, nqc = qt.shape
    tq = nqc // KV_GROUP
    lk = k.shape[0]
    tk = min(tk, lk)
    nk = lk // tk
    gw = KV_GROUP * HEAD_DIM

    def nxt(h, i, j):
        j1 = j + 1
        i1 = i + j1 // nk
        h1 = h + i1 // nq
        return jnp.minimum(h1, hkv - 1), i1 % nq, j1 % nk

    def qn_map(h, i, j):
        h1, i1, _ = nxt(h, i, j)
        return h1, i1, 0, 0

    def kn_map(h, i, j):
        h1, _, j1 = nxt(h, i, j)
        return j1, h1

    qspec = lambda m: pl.BlockSpec((None, None, HEAD_DIM, nqc), m)
    return pl.pallas_call(
        _flash_t_kernel,
        grid=(hkv, nq, nk),
        in_specs=[
            qspec(qn_map),
            pl.BlockSpec((tk, HEAD_DIM), kn_map),
            qspec(lambda h, i, j: (h, i, 0, 0)),
            pl.BlockSpec((tk, HEAD_DIM), lambda h, i, j: (0, h)),
            pl.BlockSpec((HEAD_DIM, tk), lambda h, i, j: (h, j)),
        ],
        out_specs=pl.BlockSpec((tq, gw), lambda h, i, j: (i, h)),
        out_shape=jax.ShapeDtypeStruct((nq * tq, hkv * gw), BF16),
        scratch_shapes=[pltpu.VMEM((tk, nqc), F32), pltpu.VMEM((tk, nqc), F32),
                        pltpu.VMEM((1, nqc), F32), pltpu.VMEM((1, nqc), F32),
                        pltpu.VMEM((1, nqc), F32), pltpu.VMEM((1, nqc), F32),
                        pltpu.VMEM((HEAD_DIM, nqc), F32)],
        compiler_params=_params("arbitrary", "arbitrary", "arbitrary"),
        name="flash_t",
    )(qt, k, qt, k, vt)


def _mixer_kernel(pc_ref, pp_ref, pn_ref, cb_ref, cc_ref, ccp_ref, ccn_ref, cx_ref, cxp_ref, cxn_ref,
                  ya_ref, wp_ref, ps_ref, cw_ref, o_ref, *, seq_len):
    i = pl.program_id(0)
    last = pl.num_programs(0) - 1
    tm = pc_ref.shape[0]
    n_ext = tm + 2 * POOL_HALO
    pool_w = pc_ref.shape[1]
    grp = pool_w // len(POOL_WINDOWS)

    prev = jnp.where(i == 0, 0.0, pp_ref[...])
    nxt = jnp.where(i == last, 0.0, pn_ref[...])
    ext = jnp.concatenate([prev, pc_ref[...], nxt], axis=0)
    t = (i * tm + lax.broadcasted_iota(jnp.int32, (tm, 1), 0))

    def shift_up(x, k):
        return pltpu.roll(x, n_ext - k, axis=0)

    def shift_down(x, k):
        return pltpu.roll(x, k, axis=0)

    for gi, w in enumerate(POOL_WINDOWS):
        e = ext[:, gi * grp:(gi + 1) * grp]
        half = w // 2
        part, span = e, 1
        while span < half:
            part = part + shift_up(part, span)
            span *= 2
        win = shift_down(part, half) + part
        win = win[POOL_HALO:POOL_HALO + tm]
        lo = jnp.clip(t - half, 0, seq_len)
        hi = jnp.clip(t - half + w, 0, seq_len)
        cnt = (hi - lo).astype(F32)
        pooled = win / cnt - e[POOL_HALO:POOL_HALO + tm]
        y = jnp.dot(pooled.astype(BF16), wp_ref[gi].astype(BF16), preferred_element_type=F32)
        o_ref[:, gi * grp:(gi + 1) * grp] = (y * ps_ref[:, gi * grp:(gi + 1) * grp]).astype(o_ref.dtype)

    aw = ya_ref.shape[1]
    o_ref[:, pool_w:pool_w + aw] = ya_ref[...]

    u = cc_ref[...] * cx_ref[...]
    u_before = jnp.where(i == 0, 0.0, ccp_ref[CONV_HALO - 1:CONV_HALO, :] * cxp_ref[CONV_HALO - 1:CONV_HALO, :])
    u_after = jnp.where(i == last, 0.0, ccn_ref[0:1, :] * cxn_ref[0:1, :])
    r = lax.broadcasted_iota(jnp.int32, (tm, 1), 0)
    um1 = jnp.where(r == 0, u_before, pltpu.roll(u, 1, axis=0))
    up1 = jnp.where(r == tm - 1, u_after, pltpu.roll(u, tm - 1, axis=0))
    conv = cw_ref[0:1, :] * um1 + cw_ref[1:2, :] * u + cw_ref[2:3, :] * up1
    o_ref[:, pool_w + aw:] = (cb_ref[...] * conv).astype(o_ref.dtype)


def _mixer(p, ya, w_pool, pool_scale, conv_w, cb_col0, tm=256):
    m = p.shape[0]
    tm = min(tm, m)
    blk = pool_scale.shape[-1]
    c0 = cb_col0 // blk
    nblk_pool, nblk_conv = m // POOL_HALO, m // CONV_HALO
    rp, rc = tm // POOL_HALO, tm // CONV_HALO

    cur = lambda c: pl.BlockSpec((tm, blk), lambda i: (i, c))
    prev_p = pl.BlockSpec((POOL_HALO, blk), lambda i: (jnp.maximum(i * rp - 1, 0), 0))
    next_p = pl.BlockSpec((POOL_HALO, blk), lambda i: (jnp.minimum((i + 1) * rp, nblk_pool - 1), 0))
    prev_c = lambda c: pl.BlockSpec((CONV_HALO, blk), lambda i: (jnp.maximum(i * rc - 1, 0), c))
    next_c = lambda c: pl.BlockSpec((CONV_HALO, blk), lambda i: (jnp.minimum((i + 1) * rc, nblk_conv - 1), c))
    aw = ya.shape[1]
    full = lambda a: pl.BlockSpec(a.shape, lambda i: (0,) * a.ndim)
    ps = pool_scale.reshape(1, blk)
    return pl.pallas_call(
        functools.partial(_mixer_kernel, seq_len=m),
        grid=(m // tm,),
        in_specs=[cur(0), prev_p, next_p,
                  cur(c0), cur(c0 + 1), prev_c(c0 + 1), next_c(c0 + 1),
                  cur(c0 + 2), prev_c(c0 + 2), next_c(c0 + 2),
                  pl.BlockSpec((tm, aw), lambda i: (i, 0)),
                  full(w_pool), full(ps), full(conv_w)],
        out_specs=pl.BlockSpec((tm, 2 * blk + aw), lambda i: (i, 0)),
        out_shape=jax.ShapeDtypeStruct((m, 2 * blk + aw), BF16),
        compiler_params=_params("parallel"),
        name="mixer",
    )(p, p, p, p, p, p, p, p, p, p, ya, w_pool, ps, conv_w)


def _down(act, wd, layer, tm=1024, tn=256):
    return _proj(act, wd, layer, wd.shape[-1], tm=tm, tn=tn, a_buffers=1, out_dtype=BF16)


def _moe_tile_kernel(te_ref, nv_ref, tr_ref, a_ref, *refs, body, n_w):
    w_refs, o_ref = refs[:n_w], refs[-1]
    tm = o_ref.shape[0]
    half = tm // 2
    rows = tr_ref[pl.program_id(1)]

    @pl.when(rows > half)
    def _():
        body(a_ref, *w_refs, o_ref)

    @pl.when(jnp.logical_and(rows > 0, rows <= half))
    def _():
        body(a_ref.at[pl.ds(0, half)], *w_refs, o_ref.at[pl.ds(0, half)])
        o_ref[pl.ds(half, tm - half), :] = jnp.zeros((tm - half, o_ref.shape[1]), o_ref.dtype)

    @pl.when(rows == 0)
    def _():
        o_ref[...] = jnp.zeros(o_ref.shape, o_ref.dtype)


_moe_gate_up_kernel = functools.partial(_moe_tile_kernel, body=_gate_up_kernel, n_w=2)
_moe_down_kernel = functools.partial(_moe_tile_kernel, body=_proj_kernel, n_w=1)


def _moe_call(kernel, a, weights, tile_expert, n_valid, tile_rows, n_out, out_dtype, tm, tn, name,
              prev=None, tile0=0, r_total=None):
    r, k = a.shape
    n_tiles = r // tm
    row = lambda j, t, te, nv, tr: (jnp.clip(t, 0, jnp.maximum(nv[0] - 1, 0)), 0)
    wspec = pl.BlockSpec((None, k, tn), lambda j, t, te, nv, tr: (te[t], 0, j))
    in_specs = [pl.BlockSpec((tm, k), row)] + [wspec] * len(weights)
    args = [tile_expert, n_valid, tile_rows, a, *weights]
    aliases = {}
    if prev is not None:
        in_specs.append(pl.BlockSpec(memory_space=pl.ANY))
        aliases = {len(args): 0}
        args.append(prev)
    return pl.pallas_call(
        kernel,
        grid_spec=pltpu.PrefetchScalarGridSpec(
            num_scalar_prefetch=3,
            grid=(n_out // tn, n_tiles),
            in_specs=in_specs,
            out_specs=pl.BlockSpec((tm, tn), lambda j, t, te, nv, tr: (t + tile0, j)),
        ),
        out_shape=jax.ShapeDtypeStruct((r_total or r, n_out), out_dtype),
        input_output_aliases=aliases,
        compiler_params=_params("parallel", "arbitrary"),
        name=name,
    )(*args)


def _moe_ffn(h, ridx, wg, wu, wd, tm=512, tn=512, tn_down=1024):
    m, d = h.shape
    n_slots = m * TOP_K
    r_max = n_slots + N_EXPERTS * tm
    n_tiles = r_max // tm
    e_flat = ridx[:, :TOP_K].reshape(n_slots)
    order = jnp.argsort(e_flat, stable=True).astype(jnp.int32)
    counts = jnp.sum((e_flat[:, None] == jnp.arange(N_EXPERTS)[None, :]).astype(jnp.int32), axis=0)
    padded = ((counts + tm - 1) // tm) * tm
    gend = jnp.cumsum(padded)
    gstart = gend - padded
    cstart = jnp.cumsum(counts) - counts
    n_valid = (gend[-1] // tm).astype(jnp.int32).reshape(1)
    tile_expert = jnp.sum((gend[None, :] <= (jnp.arange(n_tiles) * tm)[:, None]).astype(jnp.int32), axis=1)
    tile_expert = jnp.minimum(tile_expert, N_EXPERTS - 1)
    last_e = tile_expert[jnp.maximum(n_valid[0] - 1, 0)]
    tile_expert = jnp.where(jnp.arange(n_tiles) < n_valid[0], tile_expert, last_e)
    rows = jnp.arange(r_max)
    row_e = tile_expert[rows // tm]
    off = rows - gstart[row_e]
    row_valid = (off < counts[row_e]) & (rows < gend[-1])
    src = jnp.clip(cstart[row_e] + off, 0, n_slots - 1)
    row_token = jnp.where(row_valid, order[src] // TOP_K, rows % m)
    tile_lo = jnp.arange(n_tiles) * tm
    tile_rows = jnp.clip(gstart[tile_expert] + counts[tile_expert] - tile_lo, 0, tm)
    tile_rows = jnp.where(jnp.arange(n_tiles) < n_valid[0], tile_rows, 0).astype(jnp.int32)
    rank = jnp.argsort(order).astype(jnp.int32)
    slot_row = (gstart[e_flat] + rank - cstart[e_flat]).reshape(m, TOP_K)

    first = n_tiles // DISPATCH_FIRST_DIV
    act = None
    for t0, ct in ((0, first), (first, n_tiles - first)):
        hs = h.at[row_token[t0 * tm:(t0 + ct) * tm]].get(mode="promise_in_bounds")
        nv_c = jnp.clip(n_valid - t0, 0, ct)
        act = _moe_call(_moe_gate_up_kernel, hs, (wg, wu), tile_expert[t0:t0 + ct], nv_c, tile_rows[t0:t0 + ct],
                        wg.shape[-1], BF16, tm, tn, "moe_gate_up", prev=act, tile0=t0, r_total=r_max)
    ys = _moe_call(_moe_down_kernel, act, (wd,), tile_expert, n_valid, tile_rows, d, BF16, tm, tn_down,
                   "moe_down")
    return ys, slot_row


def _combine_kernel(x_ref, y1_ref, y2_ref, gw_ref, gate_ref, g_ref, o_ref):
    gw = gw_ref[...]
    y = gw[:, 0:1] * y1_ref[...].astype(F32) + gw[:, 1:2] * y2_ref[...].astype(F32)
    o_ref[...] = _rms(x_ref[...] + gate_ref[...] * y, g_ref[...])


def _combine_final(x, y1, y2, gw, gate, g, tm=256):
    m, d = x.shape
    row = pl.BlockSpec((tm, d), lambda i: (i, 0))
    vec = pl.BlockSpec((1, d), lambda i: (0, 0))
    return pl.pallas_call(
        _combine_kernel,
        grid=(m // tm,),
        in_specs=[row, row, row, pl.BlockSpec((tm, LANES), lambda i: (i, 0)), vec, vec],
        out_specs=row,
        out_shape=jax.ShapeDtypeStruct((m, d), F32),
        compiler_params=_params("parallel"),
        name="combine_final",
    )(x, y1, y2, gw, gate, g)


def _rope_tables(seq_len):
    rows = seq_len // GRID_W
    row = jnp.repeat(jnp.arange(rows), GRID_W).astype(F32)
    col = jnp.tile(jnp.arange(GRID_W), rows).astype(F32)
    n_axis = HEAD_DIM // 4
    inv = ROPE_BASE ** (-jnp.arange(n_axis, dtype=F32) / n_axis)
    ang = jnp.concatenate([row[:, None] * inv, col[:, None] * inv], axis=-1)
    cos, sin = jnp.cos(ang), jnp.sin(ang)
    return jnp.concatenate([cos, cos], axis=-1), jnp.concatenate([-sin, sin], axis=-1)


def kernel(x, c, ctx, c_ctx, w_mod, b_mod, g_mix, w_in, w_pool, pool_scale, g_q, g_k, conv_w, w_out, g_ffn,
           w_gate_dense, w_up_dense, w_down_dense, w_router, b_router, w_gate_exp, w_up_exp, w_down_exp, g_final):
    batch, seq, d = x.shape
    depth = w_mod.shape[0]
    assert batch == 1 and depth == 2
    n_ctx = ctx.shape[1]
    pool_w = pool_scale.shape[-1]
    conv_wd = conv_w.shape[-1]
    q_w = (d // (2 * HEAD_DIM)) * HEAD_DIM
    kv_w = q_w // KV_GROUP
    q_off = pool_w
    k_off = q_off + q_w
    cb_off = k_off + 2 * kv_w
    in_w = cb_off + 3 * conv_wd

    xs, cs = x[0], ctx[0]
    cvec = jnp.zeros((SUBLANES, d), F32).at[0].set(c[0]).at[1].set(c_ctx)
    mods = _mods(cvec, w_mod, b_mod)

    def mod(l, row, k):
        return mods[l, row:row + 1, k * d:(k + 1) * d]

    cosf, sinf = _rope_tables(seq)
    cos_c, sin_c = jnp.ones((n_ctx, HEAD_DIM), F32), jnp.zeros((n_ctx, HEAD_DIM), F32)
    qscale = HEAD_DIM ** -0.5 * LOG2E

    pend_x = pend_c = None
    for l in range(depth):
        last = l == depth - 1
        vec = lambda a: a[l].reshape(1, -1)
        gq, gk = vec(g_q) * qscale, vec(g_k)

        def first_norm(stream, pend, row):
            if pend is None:
                (h,) = _norm(stream, vec(g_mix), mod(l, row, 0), mod(l, row, 1))
                return stream, h
            return _norm(stream, vec(g_mix), mod(l, row, 0), mod(l, row, 1), resid=pend)

        cs, hc = first_norm(cs, pend_c, 1)
        if last:
            pkv = _proj(hc, w_in, l, 2 * kv_w, col0=k_off)
            kc, vtc = _qkv_prep(pkv, cos_c, sin_c, gq, gk, None, 0)
        else:
            pc = _proj(hc, w_in, l, in_w)
            qtc, kc, vtc = _qkv_prep(pc, cos_c, sin_c, gq, gk, q_off, k_off)
            yac = _flash_t(qtc, kc, vtc)
            mixc = _mixer(pc, yac, w_pool[l], pool_scale[l], conv_w[l], cb_off)
            ymc = _proj(mixc, w_out, l, d, out_dtype=BF16)

        xs, hx = first_norm(xs, pend_x, 0)
        px = _proj(hx, w_in, l, in_w)
        qtx, kx, vtx = _qkv_prep(px, cosf, sinf, gq, gk, q_off, k_off)
        k_all = jnp.concatenate([kc, kx], axis=0)
        vt_all = jnp.concatenate([vtc, vtx], axis=1)
        yax = _flash_t(qtx, k_all, vt_all)
        mixx = _mixer(px, yax, w_pool[l], pool_scale[l], conv_w[l], cb_off)
        ymx = _proj(mixx, w_out, l, d, out_dtype=BF16)

        i = l // 2
        if l % 2 == 0:
            def ffn(stream, ym, row):
                xn, h = _norm(stream, vec(g_ffn), mod(l, row, 3), mod(l, row, 4), resid=(ym, mod(l, row, 2)))
                act = _gate_up(h, w_gate_dense, w_up_dense, i)
                return xn, _down(act, w_down_dense, i)

            xs, yfx = ffn(xs, ymx, 0)
            pend_x = (yfx, mod(l, 0, 5))
            if not last:
                cs, yfc = ffn(cs, ymc, 1)
                pend_c = (yfc, mod(l, 1, 5))
        else:
            wr = jnp.zeros((d, LANES), F32).at[:, :N_EXPERTS].set(w_router[i])
            br = jnp.full((1, LANES), NEG_BIG, F32).at[0, :N_EXPERTS].set(b_router[i])
            xs, hf, ridx, rgw = _norm(xs, vec(g_ffn), mod(l, 0, 3), mod(l, 0, 4), resid=(ymx, mod(l, 0, 2)),
                                      router=(wr, br), mode="route")
            ys, slot_row = _moe_ffn(hf, ridx, w_gate_exp[i], w_up_exp[i], w_down_exp[i])
            y1 = ys.at[slot_row[:, 0]].get(mode="promise_in_bounds")
            y2 = ys.at[slot_row[:, 1]].get(mode="promise_in_bounds")
            pend_x = (y1, y2, rgw, mod(l, 0, 5))

    if len(pend_x) == 4:
        out = _combine_final(xs, pend_x[0], pend_x[1], pend_x[2], pend_x[3], g_final.reshape(1, -1))
    else:
        (out,) = _norm(xs, g_final.reshape(1, -1), resid=pend_x, mode="final")
    return out[None]
```
